```python
import jax, jax.numpy as jnp
from jax import lax
import numpy as np

D_MODEL = 4096
BATCH = 4
SEQ = 2048
DEPTH = 1
DEC_BATCH = 128
DEC_SEQ = 1
PAST_LEN = 16384
PAGE_SIZE = 128

MIX_W = D_MODEL
POOL_W = MIX_W // 2
CONV_W = MIX_W - POOL_W
POOL_WINDOWS = (2, 4, 8, 16)
N_POOL_GROUPS = len(POOL_WINDOWS)
POOL_GROUP = POOL_W // N_POOL_GROUPS
POOL_BUF = max(POOL_WINDOWS) - 1
CONV_WIDTH = 31
CONV_BUF = CONV_WIDTH - 1
CONV_HEADS = 4
CONV_HEAD_DIM = CONV_W // CONV_HEADS
N_EXPERTS = 32
TOP_K = 4
D_FF = D_MODEL
SWIGLU_LIMIT = 7.0
SWIGLU_ALPHA = 1.702
MOE_BLOCK = 128
N_MOD = 6
EPS = 1e-5

kernel_name = "hymba_pool_conformer_moe_adaln_step"


def rmsnorm(x, g):
    xf = x.astype(jnp.float32)
    y = xf * lax.rsqrt(jnp.mean(xf * xf, axis=-1, keepdims=True) + EPS)
    return (y * g.astype(jnp.float32)).astype(x.dtype)


def head_layernorm(y, g, b):
    B, L, C = y.shape
    yf = y.astype(jnp.float32).reshape(B, L, CONV_HEADS, CONV_HEAD_DIM)
    mu = jnp.mean(yf, axis=-1, keepdims=True)
    d = yf - mu
    var = jnp.mean(d * d, axis=-1, keepdims=True)
    yn = (d * lax.rsqrt(var + EPS)).reshape(B, L, C)
    return (yn * g.astype(jnp.float32) + b.astype(jnp.float32)).astype(y.dtype)


def pool_mixer(u, buf, start_pos, w_pool, pool_scale):
    B, L, C = u.shape
    ext = jnp.concatenate([buf.astype(u.dtype), u], axis=1)
    cs = jnp.cumsum(ext.astype(jnp.float32), axis=1)
    cs0 = jnp.concatenate([jnp.zeros((B, 1, C), jnp.float32), cs], axis=1)
    pos = (jnp.arange(L) + start_pos).astype(jnp.float32)[:, None]
    uf = u.astype(jnp.float32)
    outs = []
    for gi, w in enumerate(POOL_WINDOWS):
        sl = slice(gi * POOL_GROUP, (gi + 1) * POOL_GROUP)
        hi = cs0[:, POOL_BUF + 1:POOL_BUF + 1 + L, sl]
        lo = cs0[:, POOL_BUF + 1 - w:POOL_BUF + 1 - w + L, sl]
        cnt = jnp.minimum(jnp.float32(w), pos + 1.0)
        d = (hi - lo) / cnt - uf[..., sl]
        outs.append(jnp.einsum('blc,cd->bld', d.astype(u.dtype), w_pool[gi]))
    y = jnp.concatenate(outs, axis=-1) * pool_scale
    return y, ext[:, -POOL_BUF:]


def conv_mixer(a, buf, w_conv, b_conv, ln_g, ln_b):
    C = a.shape[-1]
    ext = jnp.concatenate([buf.astype(a.dtype), a], axis=1)
    y = lax.conv_general_dilated(
        ext, w_conv.astype(a.dtype)[:, None, :], window_strides=(1,), padding='VALID',
        dimension_numbers=('NWC', 'WIO', 'NWC'), feature_group_count=C)
    y = y + b_conv
    y = jax.nn.silu(head_layernorm(y, ln_g, ln_b))
    return y, ext[:, -CONV_BUF:]


def moe(h, w_router, b_router, w_gu, b_gu, w_down, b_down):
    B, L, D = h.shape
    t = h.reshape(-1, D)
    T = t.shape[0]
    logits = (t @ w_router + b_router).astype(jnp.float32)
    top_val, top_idx = lax.top_k(logits, TOP_K)
    gates = jax.nn.softmax(top_val, axis=-1).astype(h.dtype)
    n_assign = T * TOP_K
    e_flat = top_idx.reshape(-1)
    g_flat = gates.reshape(-1)
    tok_flat = jnp.repeat(jnp.arange(T, dtype=jnp.int32), TOP_K)
    order = jnp.argsort(e_flat)
    e_s, tok_s, g_s = e_flat[order], tok_flat[order], g_flat[order]
    counts = jnp.zeros((N_EXPERTS,), jnp.int32).at[e_flat].add(1)
    starts = jnp.cumsum(counts) - counts
    padded = (counts + MOE_BLOCK - 1) // MOE_BLOCK * MOE_BLOCK
    pends = jnp.cumsum(padded)
    pstarts = pends - padded
    dest = pstarts[e_s] + (jnp.arange(n_assign, dtype=jnp.int32) - starts[e_s])
    n_blocks = -(-n_assign // MOE_BLOCK) + N_EXPERTS
    P = n_blocks * MOE_BLOCK
    slot_tok = jnp.zeros((P,), jnp.int32).at[dest].set(tok_s)
    slot_gate = jnp.zeros((P,), h.dtype).at[dest].set(g_s)
    block_start = jnp.arange(n_blocks, dtype=jnp.int32) * MOE_BLOCK
    block_exp = jnp.minimum(jnp.sum(block_start[:, None] >= pends[None, :], axis=1), N_EXPERTS - 1)

    def run(args):
        tok_b, gate_b, e = args
        xb = t[tok_b]
        gu = xb @ w_gu[e] + b_gu[e]
        g = jnp.minimum(gu[:, :D_FF], SWIGLU_LIMIT)
        u = jnp.clip(gu[:, D_FF:], -SWIGLU_LIMIT, SWIGLU_LIMIT)
        act = (u + 1.0) * (g * jax.nn.sigmoid(SWIGLU_ALPHA * g))
        y = act @ w_down[e] + b_down[e]
        return y * gate_b[:, None]

    ys = lax.map(run, (slot_tok.reshape(n_blocks, MOE_BLOCK),
                       slot_gate.reshape(n_blocks, MOE_BLOCK), block_exp))
    out = jnp.zeros_like(t).at[slot_tok].add(ys.reshape(P, D))
    return out.reshape(B, L, D)


def trunk_layer(x, c, pool_buf, conv_buf, start_pos, w_ada, b_ada, g_norm1, w_in, w_pool,
                pool_scale, w_conv, b_conv, ln_g, ln_b, w_out, g_norm2, w_router, b_router,
                w_gu, b_gu, w_down, b_down):
    mod = (jax.nn.silu(c) @ w_ada + b_ada)[:, None, :]
    sh1, sc1, ga1, sh2, sc2, ga2 = jnp.split(mod, N_MOD, axis=-1)
    h = rmsnorm(x, g_norm1) * (1.0 + sc1) + sh1
    z = h @ w_in
    u = z[..., :POOL_W]
    a = z[..., POOL_W:POOL_W + CONV_W] * jax.nn.sigmoid(z[..., POOL_W + CONV_W:])
    pool_out, new_pool = pool_mixer(u, pool_buf, start_pos, w_pool, pool_scale)
    conv_out, new_conv = conv_mixer(a, conv_buf, w_conv, b_conv, ln_g, ln_b)
    mix = jnp.concatenate([pool_out, conv_out], axis=-1) @ w_out
    x = x + ga1 * mix
    h2 = rmsnorm(x, g_norm2) * (1.0 + sc2) + sh2
    x = x + ga2 * moe(h2, w_router, b_router, w_gu, b_gu, w_down, b_down)
    return x, new_pool, new_conv


def setup_inputs(seed: int = 0) -> dict:
    key = jax.random.key(seed)
    ks = jax.random.split(key, 32)
    f32 = jnp.float32

    def nrm(k, shape, scale):
        return jax.random.normal(k, shape, f32) * scale

    D = D_MODEL
    return {
        "x_prompt": nrm(ks[0], (BATCH, SEQ, D), 1.0),
        "x_sample": nrm(ks[1], (DEC_BATCH, DEC_SEQ, D), 1.0),
        "state_pool": nrm(ks[2], (DEPTH, DEC_BATCH, POOL_BUF, POOL_W), 1.0),
        "state_conv": nrm(ks[3], (DEPTH, DEC_BATCH, CONV_BUF, CONV_W), 0.5),
        "c_prompt": nrm(ks[4], (BATCH, D), 1.0),
        "c_sample": nrm(ks[5], (DEC_BATCH, D), 1.0),
        "w_ada": nrm(ks[6], (DEPTH, D, N_MOD * D), 0.5 * D ** -0.5),
        "b_ada": nrm(ks[7], (DEPTH, N_MOD * D), 0.02),
        "g_norm1": 1.0 + nrm(ks[8], (DEPTH, D), 0.02),
        "w_in": nrm(ks[9], (DEPTH, D, POOL_W + 2 * CONV_W), D ** -0.5),
        "w_pool": nrm(ks[10], (DEPTH, N_POOL_GROUPS, POOL_GROUP, POOL_GROUP), POOL_GROUP ** -0.5),
        "pool_scale": 1.0 + nrm(ks[11], (DEPTH, POOL_W), 0.02),
        "w_conv": nrm(ks[12], (DEPTH, CONV_WIDTH, CONV_W), CONV_WIDTH ** -0.5),
        "b_conv": nrm(ks[13], (DEPTH, CONV_W), 0.02),
        "ln_g": 1.0 + nrm(ks[14], (DEPTH, CONV_W), 0.02),
        "ln_b": nrm(ks[15], (DEPTH, CONV_W), 0.02),
        "w_out": nrm(ks[16], (DEPTH, MIX_W, D), MIX_W ** -0.5),
        "g_norm2": 1.0 + nrm(ks[17], (DEPTH, D), 0.02),
        "w_router": nrm(ks[18], (DEPTH, D, N_EXPERTS), D ** -0.5),
        "b_router": nrm(ks[19], (DEPTH, N_EXPERTS), 0.01),
        "w_gu": nrm(ks[20], (DEPTH, N_EXPERTS, D, 2 * D_FF), D ** -0.5),
        "b_gu": nrm(ks[21], (DEPTH, N_EXPERTS, 2 * D_FF), 0.02),
        "w_down": nrm(ks[22], (DEPTH, N_EXPERTS, D_FF, D), D_FF ** -0.5),
        "b_down": nrm(ks[23], (DEPTH, N_EXPERTS, D), 0.02),
        "g_final": 1.0 + nrm(ks[24], (D,), 0.02),
    }


def reference(x_prompt, x_sample, state_pool, state_conv, c_prompt, c_sample, w_ada, b_ada,
              g_norm1, w_in, w_pool, pool_scale, w_conv, b_conv, ln_g, ln_b, w_out, g_norm2,
              w_router, b_router, w_gu, b_gu, w_down, b_down, g_final):
    xp, xs = x_prompt, x_sample
    pool_p, conv_p, pool_s, conv_s = [], [], [], []
    for l in range(DEPTH):
        lw = (w_ada[l], b_ada[l], g_norm1[l], w_in[l], w_pool[l], pool_scale[l], w_conv[l],
              b_conv[l], ln_g[l], ln_b[l], w_out[l], g_norm2[l], w_router[l], b_router[l],
              w_gu[l], b_gu[l], w_down[l], b_down[l])
        zp = jnp.zeros((xp.shape[0], POOL_BUF, POOL_W), xp.dtype)
        zc = jnp.zeros((xp.shape[0], CONV_BUF, CONV_W), xp.dtype)
        xp, npool, nconv = trunk_layer(xp, c_prompt, zp, zc, 0, *lw)
        pool_p.append(npool)
        conv_p.append(nconv)
        xs, npool, nconv = trunk_layer(xs, c_sample, state_pool[l], state_conv[l], PAST_LEN, *lw)
        pool_s.append(npool)
        conv_s.append(nconv)
    y_prompt = rmsnorm(xp, g_final)
    y_sample = rmsnorm(xs, g_final)
    new_pool_prompt = jnp.stack(pool_p, axis=0)
    new_conv_prompt = jnp.stack(conv_p, axis=0)
    new_pool_sample = jnp.stack(pool_s, axis=0)
    new_conv_sample = jnp.stack(conv_s, axis=0)
    return (y_prompt, y_sample, new_pool_prompt, new_conv_prompt, new_pool_sample, new_conv_sample)
```

```python
import functools

import jax
import jax.numpy as jnp
from jax import lax
from jax.experimental import pallas as pl
from jax.experimental.pallas import tpu as pltpu

F32 = jnp.float32
BF16 = jnp.bfloat16
I32 = jnp.int32

D_MODEL = 4096
POOL_W = 2048
CONV_W = 2048
POOL_WINDOWS = (2, 4, 8, 16)
POOL_GROUP = 512
POOL_BUF = 15
CONV_WIDTH = 31
CONV_BUF = 30
CONV_HEADS = 4
CONV_HEAD_DIM = 512
N_EXPERTS = 32
TOP_K = 4
D_FF = 4096
SWIGLU_LIMIT = 7.0
SWIGLU_ALPHA = 1.702
N_MOD = 6
EPS = 1e-5

MIB = 1024 * 1024

MOE_TM = 1280
MOE_SUB = 256
MOE_NSUB = MOE_TM // MOE_SUB
MOE_TF = 256
MOE_TN = 512
GATHER_ROWS = 256
COMBINE_TOK = GATHER_ROWS // TOP_K


def _params(vmem_mib, n_axes):
    return pltpu.CompilerParams(
        dimension_semantics=("arbitrary",) * n_axes,
        vmem_limit_bytes=vmem_mib * MIB,
    )


def _ada_kernel(c_ref, w_ref, b_ref, o_ref):
    c = c_ref[...]
    s = (c * jax.nn.sigmoid(c)).astype(BF16)
    o_ref[...] = jnp.dot(s, w_ref[...].astype(BF16), preferred_element_type=F32) + b_ref[...]


def _ada(c_all, w_ada, b_ada):
    m, d = c_all.shape
    n = w_ada.shape[1]
    tn = 512
    return pl.pallas_call(
        _ada_kernel,
        grid=(n // tn,),
        in_specs=[
            pl.BlockSpec((m, d), lambda j: (0, 0)),
            pl.BlockSpec((d, tn), lambda j: (0, j)),
            pl.BlockSpec((1, tn), lambda j: (0, j)),
        ],
        out_specs=pl.BlockSpec((m, tn), lambda j: (0, j)),
        out_shape=jax.ShapeDtypeStruct((m, n), F32),
        compiler_params=_params(40, 1),
        name="ada",
    )(c_all, w_ada, b_ada.reshape(1, n))


def _rmsnorm(x, g):
    return (x * lax.rsqrt(jnp.mean(x * x, axis=-1, keepdims=True) + EPS)) * g


def _in_proj_kernel(x_ref, g_ref, sc_ref, sh_ref, w_ref, z_ref, h_scr):
    @pl.when(pl.program_id(1) == 0)
    def _():
        h = _rmsnorm(x_ref[...], g_ref[...]) * (1.0 + sc_ref[...]) + sh_ref[...]
        h_scr[...] = h.astype(BF16)

    z_ref[...] = jnp.dot(h_scr[...], w_ref[...], preferred_element_type=F32)


def _mod_spec(per_row, tm, tn, rows_per_seq, col_of):
    if per_row:
        return pl.BlockSpec((tm, tn), lambda i, j: (i, col_of(j)))
    return pl.BlockSpec((None, 1, tn), lambda i, j: ((i * tm) // rows_per_seq, 0, col_of(j)))


def _in_proj(x, g, sc, sh, w_bf, *, tm, rows_per_seq, per_row):
    m, d = x.shape
    n = w_bf.shape[1]
    tn = 512
    zero = lambda j: 0
    return pl.pallas_call(
        _in_proj_kernel,
        grid=(m // tm, n // tn),
        in_specs=[
            pl.BlockSpec((tm, d), lambda i, j: (i, 0)),
            pl.BlockSpec((1, d), lambda i, j: (0, 0)),
            _mod_spec(per_row, tm, d, rows_per_seq, zero),
            _mod_spec(per_row, tm, d, rows_per_seq, zero),
            pl.BlockSpec((d, tn), lambda i, j: (0, j)),
        ],
        out_specs=pl.BlockSpec((tm, tn), lambda i, j: (i, j)),
        out_shape=jax.ShapeDtypeStruct((m, n), F32),
        scratch_shapes=[pltpu.VMEM((tm, d), BF16)],
        compiler_params=_params(56, 2),
        name="in_proj",
    )(x, g, sc, sh, w_bf)


def _pool_counts(pos, w):
    return jnp.minimum(jnp.float32(w), pos + 1.0)


def _layernorm_silu(y, g, b):
    mu = jnp.mean(y, axis=-1, keepdims=True)
    dlt = y - mu
    var = jnp.mean(dlt * dlt, axis=-1, keepdims=True)
    yn = dlt * lax.rsqrt(var + EPS) * g + b
    return yn * jax.nn.sigmoid(yn)


def _mixer_seq_kernel(z_ref, pbuf_ref, cbuf_ref, wpool_ref, pscale_ref, wconv_ref, bconv_ref,
                      lng_ref, lnb_ref, mix_ref, npool_ref, nconv_ref, extu, exta, *,
                      tl, start_pos):
    i = pl.program_id(1)
    hu, ha = POOL_BUF + 1, CONV_BUF + 2

    @pl.when(i == 0)
    def _():
        extu[1:hu, :] = pbuf_ref[...]
        exta[2:ha, :] = cbuf_ref[...]

    @pl.when(i > 0)
    def _():
        extu[0:hu, :] = extu[tl:tl + hu, :]
        exta[0:ha, :] = exta[tl:tl + ha, :]

    extu[hu:hu + tl, :] = z_ref[:, 0:POOL_W]
    exta[ha:ha + tl, :] = (z_ref[:, POOL_W:POOL_W + CONV_W]
                           * jax.nn.sigmoid(z_ref[:, POOL_W + CONV_W:POOL_W + 2 * CONV_W]))

    pos = (lax.broadcasted_iota(I32, (tl, 1), 0) + (i * tl + start_pos)).astype(F32)
    for gi, w in enumerate(POOL_WINDOWS):
        sl = slice(gi * POOL_GROUP, (gi + 1) * POOL_GROUP)
        u = extu[hu:hu + tl, sl]
        s = u
        for j in range(1, w):
            s = s + extu[hu - j:hu - j + tl, sl]
        dlt = s * (1.0 / _pool_counts(pos, w)) - u
        y = jnp.dot(dlt.astype(BF16), wpool_ref[gi], preferred_element_type=F32)
        mix_ref[:, sl] = (y * pscale_ref[:, sl]).astype(BF16)

    for hi in range(CONV_HEADS):
        sl = slice(hi * CONV_HEAD_DIM, (hi + 1) * CONV_HEAD_DIM)
        acc = exta[2:2 + tl, sl] * wconv_ref[0:1, sl]
        for j in range(1, CONV_WIDTH):
            acc = acc + exta[2 + j:2 + j + tl, sl] * wconv_ref[j:j + 1, sl]
        y = _layernorm_silu(acc + bconv_ref[:, sl], lng_ref[:, sl], lnb_ref[:, sl])
        mix_ref[:, POOL_W + hi * CONV_HEAD_DIM:POOL_W + (hi + 1) * CONV_HEAD_DIM] = y.astype(BF16)

    @pl.when(i == pl.num_programs(1) - 1)
    def _():
        npool_ref[...] = extu[tl + 1:tl + hu, :]
        nconv_ref[...] = exta[tl + 2:tl + ha, :]


def _mixer_seq(z, pbuf, cbuf, wpool_bf, pscale, wconv, bconv, lng, lnb, *, seq_len, start_pos):
    m = z.shape[0]
    nb = m // seq_len
    tl = 256
    nt = seq_len // tl
    row = lambda a: a.reshape(1, -1)
    full2 = lambda r, c: pl.BlockSpec((r, c), lambda b, i: (0, 0))
    kern = functools.partial(_mixer_seq_kernel, tl=tl, start_pos=start_pos)
    return pl.pallas_call(
        kern,
        grid=(nb, nt),
        in_specs=[
            pl.BlockSpec((tl, z.shape[1]), lambda b, i: (b * nt + i, 0)),
            pl.BlockSpec((None, POOL_BUF, POOL_W), lambda b, i: (b, 0, 0)),
            pl.BlockSpec((None, CONV_BUF, CONV_W), lambda b, i: (b, 0, 0)),
            pl.BlockSpec((len(POOL_WINDOWS), POOL_GROUP, POOL_GROUP), lambda b, i: (0, 0, 0)),
            full2(1, POOL_W),
            full2(CONV_WIDTH, CONV_W),
            full2(1, CONV_W),
            full2(1, CONV_W),
            full2(1, CONV_W),
        ],
        out_specs=[
            pl.BlockSpec((tl, D_MODEL), lambda b, i: (b * nt + i, 0)),
            pl.BlockSpec((None, POOL_BUF, POOL_W), lambda b, i: (b, 0, 0)),
            pl.BlockSpec((None, CONV_BUF, CONV_W), lambda b, i: (b, 0, 0)),
        ],
        out_shape=[
            jax.ShapeDtypeStruct((m, D_MODEL), BF16),
            jax.ShapeDtypeStruct((nb, POOL_BUF, POOL_W), F32),
            jax.ShapeDtypeStruct((nb, CONV_BUF, CONV_W), F32),
        ],
        scratch_shapes=[
            pltpu.VMEM((POOL_BUF + 1 + tl, POOL_W), F32),
            pltpu.VMEM((CONV_BUF + 2 + tl, CONV_W), F32),
        ],
        compiler_params=_params(40, 2),
        name="mixer_seq",
    )(z, pbuf, cbuf, wpool_bf, row(pscale), wconv, row(bconv), row(lng), row(lnb))


def _mixer_step_kernel(z_ref, pbuf_ref, cbuf_ref, wpool_ref, pscale_ref, wconv_ref, bconv_ref,
                       lng_ref, lnb_ref, mix_ref, u_ref, a_ref, *, start_pos):
    u_all = z_ref[:, 0:POOL_W]
    a_all = (z_ref[:, POOL_W:POOL_W + CONV_W]
             * jax.nn.sigmoid(z_ref[:, POOL_W + CONV_W:POOL_W + 2 * CONV_W]))
    u_ref[...] = u_all
    a_ref[...] = a_all

    for gi, w in enumerate(POOL_WINDOWS):
        sl = slice(gi * POOL_GROUP, (gi + 1) * POOL_GROUP)
        u = u_all[:, sl]
        s = u
        for j in range(1, w):
            s = s + pbuf_ref[:, POOL_BUF - j, sl]
        cnt = min(float(w), float(start_pos) + 1.0)
        dlt = s / cnt - u
        y = jnp.dot(dlt.astype(BF16), wpool_ref[gi], preferred_element_type=F32)
        mix_ref[:, sl] = (y * pscale_ref[:, sl]).astype(BF16)

    for hi in range(CONV_HEADS):
        sl = slice(hi * CONV_HEAD_DIM, (hi + 1) * CONV_HEAD_DIM)
        acc = a_all[:, sl] * wconv_ref[CONV_BUF:CONV_BUF + 1, sl]
        for j in range(CONV_BUF):
            acc = acc + cbuf_ref[:, j, sl] * wconv_ref[j:j + 1, sl]
        y = _layernorm_silu(acc + bconv_ref[:, sl], lng_ref[:, sl], lnb_ref[:, sl])
        mix_ref[:, POOL_W + hi * CONV_HEAD_DIM:POOL_W + (hi + 1) * CONV_HEAD_DIM] = y.astype(BF16)


def _mixer_step(z, pbuf, cbuf, wpool_bf, pscale, wconv, bconv, lng, lnb, *, start_pos):
    m = z.shape[0]
    bt = 16
    row = lambda a: a.reshape(1, -1)
    full2 = lambda r, c: pl.BlockSpec((r, c), lambda b: (0, 0))
    kern = functools.partial(_mixer_step_kernel, start_pos=start_pos)
    return pl.pallas_call(
        kern,
        grid=(m // bt,),
        in_specs=[
            pl.BlockSpec((bt, z.shape[1]), lambda b: (b, 0)),
            pl.BlockSpec((bt, POOL_BUF, POOL_W), lambda b: (b, 0, 0)),
            pl.BlockSpec((bt, CONV_BUF, CONV_W), lambda b: (b, 0, 0)),
            pl.BlockSpec((len(POOL_WINDOWS), POOL_GROUP, POOL_GROUP), lambda b: (0, 0, 0)),
            full2(1, POOL_W),
            full2(CONV_WIDTH, CONV_W),
            full2(1, CONV_W),
            full2(1, CONV_W),
            full2(1, CONV_W),
        ],
        out_specs=[
            pl.BlockSpec((bt, D_MODEL), lambda b: (b, 0)),
            pl.BlockSpec((bt, POOL_W), lambda b: (b, 0)),
            pl.BlockSpec((bt, CONV_W), lambda b: (b, 0)),
        ],
        out_shape=[
            jax.ShapeDtypeStruct((m, D_MODEL), BF16),
            jax.ShapeDtypeStruct((m, POOL_W), F32),
            jax.ShapeDtypeStruct((m, CONV_W), F32),
        ],
        compiler_params=_params(40, 1),
        name="mixer_step",
    )(z, pbuf, cbuf, wpool_bf, row(pscale), wconv, row(bconv), row(lng), row(lnb))


def _out_proj_kernel(m_ref, w_ref, x_ref, ga_ref, o_ref):
    o_ref[...] = x_ref[...] + ga_ref[...] * jnp.dot(m_ref[...], w_ref[...],
                                                    preferred_element_type=F32)


def _out_proj(mix, w_bf, x, ga, *, tm, rows_per_seq, per_row):
    m, d = mix.shape
    n = w_bf.shape[1]
    tn = 1024
    return pl.pallas_call(
        _out_proj_kernel,
        grid=(m // tm, n // tn),
        in_specs=[
            pl.BlockSpec((tm, d), lambda i, j: (i, 0)),
            pl.BlockSpec((d, tn), lambda i, j: (0, j)),
            pl.BlockSpec((tm, tn), lambda i, j: (i, j)),
            _mod_spec(per_row, tm, tn, rows_per_seq, lambda j: j),
        ],
        out_specs=pl.BlockSpec((tm, tn), lambda i, j: (i, j)),
        out_shape=jax.ShapeDtypeStruct((m, n), F32),
        compiler_params=_params(52, 2),
        name="out_proj",
    )(mix, w_bf, x, ga)


def _route_kernel(x_ref, g_ref, sc_ref, sh_ref, wr_ref, br_ref, cin_ref, h2_any, h2_ref, meta_ref,
                  gate_ref, cout_ref, run_scr, *, tm):
    del h2_any
    step = pl.program_id(0)

    @pl.when(step == 0)
    def _():
        run_scr[...] = cin_ref[...]

    h2 = _rmsnorm(x_ref[...], g_ref[...]) * (1.0 + sc_ref[...]) + sh_ref[...]
    h2_ref[...] = h2
    logits = jnp.dot(h2.astype(BF16), wr_ref[...], preferred_element_type=F32) + br_ref[...]

    e_iota = lax.broadcasted_iota(I32, (tm, N_EXPERTS), 1).astype(F32)
    work = logits
    vals, idxs, sels = [], [], []
    for _ in range(TOP_K):
        mx = jnp.max(work, axis=1, keepdims=True)
        idx = jnp.min(jnp.where(work == mx, e_iota, float(N_EXPERTS)), axis=1, keepdims=True)
        sel = e_iota == idx
        vals.append(mx)
        idxs.append(idx)
        sels.append(sel)
        work = jnp.where(sel, -jnp.inf, work)

    exps = [jnp.exp(v - vals[0]) for v in vals]
    den = exps[0] + exps[1] + exps[2] + exps[3]

    onehot = jnp.zeros((tm, N_EXPERTS), F32)
    for sel in sels:
        onehot = onehot + jnp.where(sel, 1.0, 0.0)
    r_iota = lax.broadcasted_iota(I32, (tm, tm), 0)
    c_iota = lax.broadcasted_iota(I32, (tm, tm), 1)
    lower = jnp.where(c_iota < r_iota, 1.0, 0.0).astype(BF16)
    before = jnp.dot(lower, onehot.astype(BF16), preferred_element_type=F32) + run_scr[...]

    lane = lax.broadcasted_iota(I32, (tm, 128), 1)
    meta = jnp.zeros((tm, 128), F32)
    gate = jnp.zeros((tm, 128), F32)
    for k in range(TOP_K):
        rank = jnp.sum(jnp.where(sels[k], before, 0.0), axis=1, keepdims=True)
        meta = jnp.where(lane == k, idxs[k], meta)
        meta = jnp.where(lane == TOP_K + k, rank, meta)
        gate = jnp.where(lane == k, exps[k] / den, gate)
    meta_ref[...] = meta.astype(I32)
    gate_ref[...] = gate

    run_scr[...] = run_scr[...] + jnp.sum(onehot, axis=0, keepdims=True)
    cout_ref[...] = run_scr[...]


def _route(x1, g, sc, sh, wr_bf, br, counts_in, h2_all, *, tm, rows_per_seq, per_row, row_block0):
    m, d = x1.shape
    zero = lambda j: 0
    mod = lambda: (pl.BlockSpec((tm, d), lambda i: (i, 0)) if per_row else
                   pl.BlockSpec((None, 1, d), lambda i: ((i * tm) // rows_per_seq, 0, 0)))
    kern = functools.partial(_route_kernel, tm=tm)
    return pl.pallas_call(
        kern,
        grid=(m // tm,),
        in_specs=[
            pl.BlockSpec((tm, d), lambda i: (i, 0)),
            pl.BlockSpec((1, d), lambda i: (0, 0)),
            mod(),
            mod(),
            pl.BlockSpec((d, N_EXPERTS), lambda i: (0, 0)),
            pl.BlockSpec((1, N_EXPERTS), lambda i: (0, 0)),
            pl.BlockSpec((1, N_EXPERTS), lambda i: (0, 0)),
            pl.BlockSpec(memory_space=pl.ANY),
        ],
        out_specs=[
            pl.BlockSpec((tm, d), lambda i: (row_block0 + i, 0)),
            pl.BlockSpec((tm, 128), lambda i: (i, 0)),
            pl.BlockSpec((tm, 128), lambda i: (i, 0)),
            pl.BlockSpec((1, N_EXPERTS), lambda i: (0, 0)),
        ],
        out_shape=[
            jax.ShapeDtypeStruct(h2_all.shape, F32),
            jax.ShapeDtypeStruct((m, 128), I32),
            jax.ShapeDtypeStruct((m, 128), F32),
            jax.ShapeDtypeStruct((1, N_EXPERTS), F32),
        ],
        scratch_shapes=[pltpu.VMEM((1, N_EXPERTS), F32)],
        input_output_aliases={7: 0},
        compiler_params=_params(52, 1),
        name="route",
    )(x1, g, sc, sh, wr_bf, br, counts_in, h2_all)


def _gather_kernel(valid_ref, omap_ref, tok_ref, h2_hbm, o_ref, buf, sem):
    del omap_ref
    s = pl.program_id(0)

    @pl.when(valid_ref[s] > 0)
    def _():
        def issue(r, carry):
            t = tok_ref[0, r]
            pltpu.make_async_copy(h2_hbm.at[pl.ds(t, 1), :], buf.at[pl.ds(r, 1), :], sem).start()
            return carry

        lax.fori_loop(0, GATHER_ROWS, issue, 0)
        pltpu.make_async_copy(h2_hbm.at[pl.ds(0, GATHER_ROWS), :], buf, sem).wait()
        o_ref[...] = buf[...].astype(BF16)


def _gather(valid, slot_tok, h2_all, n_steps):
    d = h2_all.shape[1]
    steps = jnp.arange(n_steps, dtype=I32)
    omap = lax.cummax(jnp.where(valid > 0, steps, 0))
    return pl.pallas_call(
        _gather_kernel,
        grid_spec=pltpu.PrefetchScalarGridSpec(
            num_scalar_prefetch=2,
            grid=(n_steps,),
            in_specs=[
                pl.BlockSpec((None, 1, GATHER_ROWS), lambda s, v, om: (s, 0, 0),
                             memory_space=pltpu.SMEM),
                pl.BlockSpec(memory_space=pl.ANY),
            ],
            out_specs=pl.BlockSpec((GATHER_ROWS, d), lambda s, v, om: (om[s], 0)),
            scratch_shapes=[pltpu.VMEM((GATHER_ROWS, d), F32), pltpu.SemaphoreType.DMA(())],
        ),
        out_shape=jax.ShapeDtypeStruct((n_steps * GATHER_ROWS, d), BF16),
        compiler_params=_params(32, 1),
        name="gather",
    )(valid, omap, slot_tok.reshape(n_steps, 1, GATHER_ROWS), h2_all)


def _moe_up_kernel(be_ref, nsub_ref, bmap_ref, xs_ref, wg_ref, wu_ref, bg_ref, bu_ref, act_ref,
                   w_scr):
    del be_ref, bmap_ref
    nsub = nsub_ref[pl.program_id(0)]
    tf = MOE_TF

    @pl.when(nsub > 0)
    def _():
        w_scr[:, 0:tf] = wg_ref[...].astype(BF16)
        w_scr[:, tf:2 * tf] = wu_ref[...].astype(BF16)

        def body(j, carry):
            rows = pl.ds(pl.multiple_of(j * MOE_SUB, MOE_SUB), MOE_SUB)
            gu = jnp.dot(xs_ref[rows, :], w_scr[...], preferred_element_type=F32)
            g = jnp.minimum(gu[:, 0:tf] + bg_ref[...], SWIGLU_LIMIT)
            u = jnp.clip(gu[:, tf:2 * tf] + bu_ref[...], -SWIGLU_LIMIT, SWIGLU_LIMIT)
            act = (u + 1.0) * (g * jax.nn.sigmoid(SWIGLU_ALPHA * g))
            act_ref[rows, :] = act.astype(BF16)
            return carry

        lax.fori_loop(0, nsub, body, 0)

        def fill(j, carry):
            rows = pl.ds(pl.multiple_of(j * MOE_SUB, MOE_SUB), MOE_SUB)
            act_ref[rows, :] = jnp.zeros((MOE_SUB, tf), BF16)
            return carry

        lax.fori_loop(nsub, MOE_NSUB, fill, 0)


def _moe_up(be, nsub, bmap, xs, w_gu, b_gu, n_blocks):
    d = xs.shape[1]
    tf = MOE_TF
    nf = D_FF // tf
    live = lambda b, ns: ns[b] > 0
    return pl.pallas_call(
        _moe_up_kernel,
        grid_spec=pltpu.PrefetchScalarGridSpec(
            num_scalar_prefetch=3,
            grid=(n_blocks, nf),
            in_specs=[
                pl.BlockSpec((MOE_TM, d), lambda b, f, be, ns, bm: (bm[b], 0)),
                pl.BlockSpec((None, d, tf),
                             lambda b, f, be, ns, bm: (be[b], 0, jnp.where(live(b, ns), f, nf - 1))),
                pl.BlockSpec((None, d, tf),
                             lambda b, f, be, ns, bm: (be[b], 0,
                                                       nf + jnp.where(live(b, ns), f, nf - 1))),
                pl.BlockSpec((None, 1, tf),
                             lambda b, f, be, ns, bm: (be[b], 0, jnp.where(live(b, ns), f, nf - 1))),
                pl.BlockSpec((None, 1, tf),
                             lambda b, f, be, ns, bm: (be[b], 0,
                                                       nf + jnp.where(live(b, ns), f, nf - 1))),
            ],
            out_specs=pl.BlockSpec(
                (MOE_TM, tf),
                lambda b, f, be, ns, bm: (jnp.where(live(b, ns), b, n_blocks),
                                          jnp.where(live(b, ns), f, 0))),
            scratch_shapes=[pltpu.VMEM((d, 2 * tf), BF16)],
        ),
        out_shape=jax.ShapeDtypeStruct(((n_blocks + 1) * MOE_TM, D_FF), BF16),
        compiler_params=_params(56, 2),
        name="moe_up",
    )(be, nsub, bmap, xs, w_gu, w_gu, b_gu, b_gu)


def _moe_down_kernel(be_ref, nsub_ref, bmap_ref, act_ref, wd_ref, bd_ref, ys_ref, w_scr):
    del be_ref, bmap_ref
    nsub = nsub_ref[pl.program_id(0)]

    @pl.when(nsub > 0)
    def _():
        w_scr[...] = wd_ref[...].astype(BF16)

        def body(j, carry):
            rows = pl.ds(pl.multiple_of(j * MOE_SUB, MOE_SUB), MOE_SUB)
            y = jnp.dot(act_ref[rows, :], w_scr[...], preferred_element_type=F32)
            ys_ref[rows, :] = y + bd_ref[...]
            return carry

        lax.fori_loop(0, nsub, body, 0)

        def fill(j, carry):
            rows = pl.ds(pl.multiple_of(j * MOE_SUB, MOE_SUB), MOE_SUB)
            ys_ref[rows, :] = jnp.zeros((MOE_SUB, MOE_TN), F32)
            return carry

        lax.fori_loop(nsub, MOE_NSUB, fill, 0)


def _moe_down(be, nsub, bmap, act, w_down, b_down, n_blocks):
    dff = act.shape[1]
    tn = MOE_TN
    nn = D_MODEL // tn
    live = lambda b, ns: ns[b] > 0
    return pl.pallas_call(
        _moe_down_kernel,
        grid_spec=pltpu.PrefetchScalarGridSpec(
            num_scalar_prefetch=3,
            grid=(n_blocks, nn),
            in_specs=[
                pl.BlockSpec((MOE_TM, dff), lambda b, n, be, ns, bm: (bm[b], 0)),
                pl.BlockSpec((None, dff, tn),
                             lambda b, n, be, ns, bm: (be[b], 0, jnp.where(live(b, ns), n, nn - 1))),
                pl.BlockSpec((None, 1, tn),
                             lambda b, n, be, ns, bm: (be[b], 0, jnp.where(live(b, ns), n, nn - 1))),
            ],
            out_specs=pl.BlockSpec(
                (MOE_TM, tn),
                lambda b, n, be, ns, bm: (jnp.where(live(b, ns), b, n_blocks),
                                          jnp.where(live(b, ns), n, 0))),
            scratch_shapes=[pltpu.VMEM((dff, tn), BF16)],
        ),
        out_shape=jax.ShapeDtypeStruct(((n_blocks + 1) * MOE_TM, D_MODEL), F32),
        compiler_params=_params(56, 2),
        name="moe_down",
    )(be, nsub, bmap, act, w_down, b_down)


def _combine_kernel(dest_ref, x_ref, gate_ref, ga_ref, gf_ref, ys_hbm, o_ref, buf, sem):
    def issue(r, carry):
        p = dest_ref[0, r]
        pltpu.make_async_copy(ys_hbm.at[pl.ds(p, 1), :], buf.at[pl.ds(r, 1), :], sem).start()
        return carry

    lax.fori_loop(0, GATHER_ROWS, issue, 0)
    pltpu.make_async_copy(ys_hbm.at[pl.ds(0, GATHER_ROWS), :], buf, sem).wait()

    acc = gate_ref[:, 0:1] * buf[0:COMBINE_TOK, :]
    for k in range(1, TOP_K):
        acc = acc + gate_ref[:, k:k + 1] * buf[k * COMBINE_TOK:(k + 1) * COMBINE_TOK, :]
    x2 = x_ref[...] + ga_ref[...] * acc
    o_ref[...] = _rmsnorm(x2, gf_ref[...])


def _combine(dest_tiles, x1, gates, ga, g_final, ys, *, rows_per_seq, per_row):
    m, d = x1.shape
    tm = COMBINE_TOK
    ga_spec = (pl.BlockSpec((tm, d), lambda i: (i, 0)) if per_row else
               pl.BlockSpec((None, 1, d), lambda i: ((i * tm) // rows_per_seq, 0, 0)))
    return pl.pallas_call(
        _combine_kernel,
        grid=(m // tm,),
        in_specs=[
            pl.BlockSpec((None, 1, GATHER_ROWS), lambda i: (i, 0, 0), memory_space=pltpu.SMEM),
            pl.BlockSpec((tm, d), lambda i: (i, 0)),
            pl.BlockSpec((tm, 128), lambda i: (i, 0)),
            ga_spec,
            pl.BlockSpec((1, d), lambda i: (0, 0)),
            pl.BlockSpec(memory_space=pl.ANY),
        ],
        out_specs=pl.BlockSpec((tm, d), lambda i: (i, 0)),
        out_shape=jax.ShapeDtypeStruct((m, d), F32),
        scratch_shapes=[pltpu.VMEM((GATHER_ROWS, d), F32), pltpu.SemaphoreType.DMA(())],
        compiler_params=_params(32, 1),
        name="combine",
    )(dest_tiles, x1, gates, ga, g_final.reshape(1, d), ys)


def _routing_tables(counts, idx, rank, n_blocks):
    nb = (counts + MOE_TM - 1) // MOE_TM
    cum_nb = jnp.cumsum(nb)
    blk_start = cum_nb - nb
    total = cum_nb[-1]
    dest = (blk_start * MOE_TM)[idx] + rank

    b_ids = jnp.arange(n_blocks, dtype=I32)
    bmap = jnp.minimum(b_ids, total - 1)
    be = jnp.minimum(jnp.sum(bmap[:, None] >= cum_nb[None, :], axis=1), N_EXPERTS - 1).astype(I32)
    rows = jnp.clip(counts[be] - (b_ids - blk_start[be]) * MOE_TM, 0, MOE_TM)
    rows = jnp.where(b_ids < total, rows, 0)
    nsub = ((rows + MOE_SUB - 1) // MOE_SUB).astype(I32)
    return dest, be, nsub, bmap.astype(I32)


def kernel(x_prompt, x_sample, state_pool, state_conv, c_prompt, c_sample, w_ada, b_ada, g_norm1,
           w_in, w_pool, pool_scale, w_conv, b_conv, ln_g, ln_b, w_out, g_norm2, w_router,
           b_router, w_gu, b_gu, w_down, b_down, g_final):
    assert w_ada.shape[0] == 1, "single-layer trunk"
    nbp, seq, d = x_prompt.shape
    nbs, dec_seq, _ = x_sample.shape
    assert dec_seq == 1
    past_len = 16384
    tp, ts = nbp * seq, nbs
    t_all = tp + ts

    n_c = nbp + nbs
    pad_c = (-n_c) % 16
    c_all = jnp.concatenate([c_prompt, c_sample, jnp.zeros((pad_c, d), F32)], axis=0)
    mod = _ada(c_all, w_ada[0], b_ada[0])
    mod_p = [mod[:nbp, k * d:(k + 1) * d].reshape(nbp, 1, d) for k in range(N_MOD)]
    mod_s = [mod[nbp:n_c, k * d:(k + 1) * d] for k in range(N_MOD)]

    w_in_bf = w_in[0].astype(BF16)
    w_out_bf = w_out[0].astype(BF16)
    w_pool_bf = w_pool[0].astype(BF16)
    wr_bf = w_router[0].astype(BF16)
    row = lambda a: a.reshape(1, -1)

    xp = x_prompt.reshape(tp, d)
    xs_tok = x_sample.reshape(ts, d)
    tm = 512

    z_p = _in_proj(xp, row(g_norm1[0]), mod_p[1], mod_p[0], w_in_bf,
                   tm=tm, rows_per_seq=seq, per_row=False)
    z_s = _in_proj(xs_tok, row(g_norm1[0]), mod_s[1], mod_s[0], w_in_bf,
                   tm=ts, rows_per_seq=1, per_row=True)
    mix_p, npool_p, nconv_p = _mixer_seq(
        z_p, jnp.zeros((nbp, POOL_BUF, POOL_W), F32), jnp.zeros((nbp, CONV_BUF, CONV_W), F32),
        w_pool_bf, pool_scale[0], w_conv[0], b_conv[0], ln_g[0], ln_b[0],
        seq_len=seq, start_pos=0)
    mix_s, u_s, a_s = _mixer_step(
        z_s, state_pool[0], state_conv[0], w_pool_bf, pool_scale[0], w_conv[0], b_conv[0],
        ln_g[0], ln_b[0], start_pos=past_len)
    x1_p = _out_proj(mix_p, w_out_bf, xp, mod_p[2], tm=tm, rows_per_seq=seq, per_row=False)
    x1_s = _out_proj(mix_s, w_out_bf, xs_tok, mod_s[2], tm=ts, rows_per_seq=1, per_row=True)

    h2_all = jnp.zeros((t_all, d), F32)
    counts0 = jnp.zeros((1, N_EXPERTS), F32)
    h2_all, meta_p, gate_p, counts1 = _route(
        x1_p, row(g_norm2[0]), mod_p[4], mod_p[3], wr_bf, row(b_router[0]), counts0, h2_all,
        tm=tm, rows_per_seq=seq, per_row=False, row_block0=0)
    h2_all, meta_s, gate_s, counts2 = _route(
        x1_s, row(g_norm2[0]), mod_s[4], mod_s[3], wr_bf, row(b_router[0]), counts1, h2_all,
        tm=ts, rows_per_seq=1, per_row=True, row_block0=tp // ts)

    n_assign = t_all * TOP_K
    n_blocks = n_assign // MOE_TM + N_EXPERTS
    counts = counts2[0].astype(I32)
    idx = jnp.concatenate([meta_p[:, 0:TOP_K], meta_s[:, 0:TOP_K]], axis=0)
    rank = jnp.concatenate([meta_p[:, TOP_K:2 * TOP_K], meta_s[:, TOP_K:2 * TOP_K]], axis=0)
    dest, be, nsub, bmap = _routing_tables(counts, idx, rank, n_blocks)

    n_gsteps = n_blocks * MOE_NSUB
    tok_ids = jnp.broadcast_to(jnp.arange(t_all, dtype=I32)[:, None], (t_all, TOP_K))
    slot_tok = jnp.zeros((n_gsteps * GATHER_ROWS,), I32).at[dest.reshape(-1)].set(
        tok_ids.reshape(-1))
    gvalid = (jnp.arange(MOE_NSUB, dtype=I32)[None, :] < nsub[:, None]).astype(I32).reshape(-1)

    xs = _gather(gvalid, slot_tok, h2_all, n_gsteps)
    act = _moe_up(be, nsub, bmap, xs, w_gu[0], b_gu[0].reshape(N_EXPERTS, 1, 2 * D_FF), n_blocks)
    ys = _moe_down(be, nsub, bmap, act, w_down[0], b_down[0].reshape(N_EXPERTS, 1, d), n_blocks)

    def dest_tiles(dst):
        n = dst.shape[0] // COMBINE_TOK
        return dst.reshape(n, COMBINE_TOK, TOP_K).transpose(0, 2, 1).reshape(n, 1, GATHER_ROWS)

    y_p = _combine(dest_tiles(dest[:tp]), x1_p, gate_p, mod_p[5], g_final, ys,
                   rows_per_seq=seq, per_row=False)
    y_s = _combine(dest_tiles(dest[tp:]), x1_s, gate_s, mod_s[5], g_final, ys,
                   rows_per_seq=1, per_row=True)

    new_pool_s = jnp.concatenate([state_pool[0][:, 1:], u_s[:, None, :]], axis=1)
    new_conv_s = jnp.concatenate([state_conv[0][:, 1:], a_s[:, None, :]], axis=1)
    return (y_p.reshape(nbp, seq, d), y_s.reshape(nbs, 1, d), npool_p[None], nconv_p[None],
            new_pool_s[None], new_conv_s[None])
```

```python
import functools

import jax
import jax.numpy as jnp
from jax import lax
from jax.experimental import pallas as pl
from jax.experimental.pallas import tpu as pltpu

F32 = jnp.float32
BF16 = jnp.bfloat16
I32 = jnp.int32

D_MODEL = 4096
POOL_W = 2048
CONV_W = 2048
POOL_WINDOWS = (2, 4, 8, 16)
POOL_GROUP = 512
POOL_BUF = 15
CONV_WIDTH = 31
CONV_BUF = 30
CONV_HEADS = 4
CONV_HEAD_DIM = 512
N_EXPERTS = 32
TOP_K = 4
D_FF = 4096
SWIGLU_LIMIT = 7.0
SWIGLU_ALPHA = 1.702
N_MOD = 6
EPS = 1e-5

MIB = 1024 * 1024

MOE_TM = 1280
MOE_SUB = 128
MOE_NSUB = MOE_TM // MOE_SUB
MOE_TF = 256
MOE_TN = 512
GATHER_ROWS = 256
COMBINE_TOK = GATHER_ROWS // TOP_K
ROW_CHUNKS = D_MODEL // 128
ROW_PITCH = ROW_CHUNKS + 4


def _params(vmem_mib, n_axes):
    return pltpu.CompilerParams(
        dimension_semantics=("arbitrary",) * n_axes,
        vmem_limit_bytes=vmem_mib * MIB,
    )


def _ada_kernel(c_ref, w_ref, b_ref, o_ref):
    c = c_ref[...]
    s = (c * jax.nn.sigmoid(c)).astype(BF16)
    o_ref[...] = jnp.dot(s, w_ref[...].astype(BF16), preferred_element_type=F32) + b_ref[...]


def _ada(c_all, w_ada, b_ada):
    m, d = c_all.shape
    n = w_ada.shape[1]
    tn = 512
    return pl.pallas_call(
        _ada_kernel,
        grid=(n // tn,),
        in_specs=[
            pl.BlockSpec((m, d), lambda j: (0, 0)),
            pl.BlockSpec((d, tn), lambda j: (0, j)),
            pl.BlockSpec((1, tn), lambda j: (0, j)),
        ],
        out_specs=pl.BlockSpec((m, tn), lambda j: (0, j)),
        out_shape=jax.ShapeDtypeStruct((m, n), F32),
        compiler_params=_params(40, 1),
        name="ada",
    )(c_all, w_ada, b_ada.reshape(1, n))


def _rmsnorm(x, g):
    return (x * lax.rsqrt(jnp.mean(x * x, axis=-1, keepdims=True) + EPS)) * g


def _in_proj_kernel(x_ref, g_ref, sc_ref, sh_ref, w_ref, z_ref, h_scr):
    @pl.when(pl.program_id(1) == 0)
    def _():
        h = _rmsnorm(x_ref[...], g_ref[...]) * (1.0 + sc_ref[...]) + sh_ref[...]
        h_scr[...] = h.astype(BF16)

    z_ref[...] = jnp.dot(h_scr[...], w_ref[...], preferred_element_type=F32)


def _mod_spec(per_row, tm, tn, rows_per_seq, col_of):
    if per_row:
        return pl.BlockSpec((tm, tn), lambda i, j: (i, col_of(j)))
    return pl.BlockSpec((None, 1, tn), lambda i, j: ((i * tm) // rows_per_seq, 0, col_of(j)))


def _in_proj(x, g, sc, sh, w_bf, *, tm, rows_per_seq, per_row):
    m, d = x.shape
    n = w_bf.shape[1]
    tn = 1024
    zero = lambda j: 0
    return pl.pallas_call(
        _in_proj_kernel,
        grid=(m // tm, n // tn),
        in_specs=[
            pl.BlockSpec((tm, d), lambda i, j: (i, 0)),
            pl.BlockSpec((1, d), lambda i, j: (0, 0)),
            _mod_spec(per_row, tm, d, rows_per_seq, zero),
            _mod_spec(per_row, tm, d, rows_per_seq, zero),
            pl.BlockSpec((d, tn), lambda i, j: (0, j)),
        ],
        out_specs=pl.BlockSpec((tm, tn), lambda i, j: (i, j)),
        out_shape=jax.ShapeDtypeStruct((m, n), F32),
        scratch_shapes=[pltpu.VMEM((tm, d), BF16)],
        compiler_params=_params(58, 2),
        name="in_proj",
    )(x, g, sc, sh, w_bf)


def _pool_counts(pos, w):
    return jnp.minimum(jnp.float32(w), pos + 1.0)


def _layernorm_silu(y, g, b):
    mu = jnp.mean(y, axis=-1, keepdims=True)
    dlt = y - mu
    var = jnp.mean(dlt * dlt, axis=-1, keepdims=True)
    yn = dlt * lax.rsqrt(var + EPS) * g + b
    return yn * jax.nn.sigmoid(yn)


def _mixer_seq_kernel(z_ref, pbuf_ref, cbuf_ref, wpool_ref, pscale_ref, wconv_ref, bconv_ref,
                      lng_ref, lnb_ref, mix_ref, npool_ref, nconv_ref, extu, exta, shf, *,
                      tl, start_pos):
    i = pl.program_id(1)
    hu, ha = POOL_BUF + 1, CONV_BUF + 2

    @pl.when(i == 0)
    def _():
        extu[0:1, :] = jnp.zeros((1, POOL_W), F32)
        exta[0:2, :] = jnp.zeros((2, CONV_W), F32)
        extu[1:hu, :] = pbuf_ref[...]
        exta[2:ha, :] = cbuf_ref[...]

    @pl.when(i > 0)
    def _():
        extu[0:hu, :] = extu[tl:tl + hu, :]
        exta[0:ha, :] = exta[tl:tl + ha, :]

    extu[hu:hu + tl, :] = z_ref[:, 0:POOL_W]
    exta[ha:ha + tl, :] = (z_ref[:, POOL_W:POOL_W + CONV_W]
                           * jax.nn.sigmoid(z_ref[:, POOL_W + CONV_W:POOL_W + 2 * CONV_W]))

    pos = (lax.broadcasted_iota(I32, (tl, 1), 0) + (i * tl + start_pos)).astype(F32)
    for gi, w in enumerate(POOL_WINDOWS):
        sl = slice(gi * POOL_GROUP, (gi + 1) * POOL_GROUP)
        u = extu[hu:hu + tl, sl]
        s = u
        for j in range(1, w):
            s = s + extu[hu - j:hu - j + tl, sl]
        dlt = s * (1.0 / _pool_counts(pos, w)) - u
        y = jnp.dot(dlt.astype(BF16), wpool_ref[gi], preferred_element_type=F32)
        mix_ref[:, sl] = (y * pscale_ref[:, sl]).astype(BF16)

    ncopy = tl + 3 * 8
    for hi in range(CONV_HEADS):
        sl = slice(hi * CONV_HEAD_DIM, (hi + 1) * CONV_HEAD_DIM)
        for b in range(1, 8):
            shf[b - 1, 0:ncopy, :] = exta[b:b + ncopy, sl]
        acc = None
        for j in range(CONV_WIDTH):
            a, b = divmod(2 + j, 8)
            if b == 0:
                src = exta[8 * a:8 * a + tl, sl]
            else:
                src = shf[b - 1, 8 * a:8 * a + tl, :]
            term = src * wconv_ref[j:j + 1, sl]
            acc = term if acc is None else acc + term
        y = _layernorm_silu(acc + bconv_ref[:, sl], lng_ref[:, sl], lnb_ref[:, sl])
        mix_ref[:, POOL_W + hi * CONV_HEAD_DIM:POOL_W + (hi + 1) * CONV_HEAD_DIM] = y.astype(BF16)

    @pl.when(i == pl.num_programs(1) - 1)
    def _():
        npool_ref[...] = extu[tl + 1:tl + hu, :]
        nconv_ref[...] = exta[tl + 2:tl + ha, :]


def _mixer_seq(z, pbuf, cbuf, wpool_bf, pscale, wconv, bconv, lng, lnb, *, seq_len, start_pos):
    m = z.shape[0]
    nb = m // seq_len
    tl = 256
    nt = seq_len // tl
    row = lambda a: a.reshape(1, -1)
    full2 = lambda r, c: pl.BlockSpec((r, c), lambda b, i: (0, 0))
    kern = functools.partial(_mixer_seq_kernel, tl=tl, start_pos=start_pos)
    return pl.pallas_call(
        kern,
        grid=(nb, nt),
        in_specs=[
            pl.BlockSpec((tl, z.shape[1]), lambda b, i: (b * nt + i, 0)),
            pl.BlockSpec((None, POOL_BUF, POOL_W), lambda b, i: (b, 0, 0)),
            pl.BlockSpec((None, CONV_BUF, CONV_W), lambda b, i: (b, 0, 0)),
            pl.BlockSpec((len(POOL_WINDOWS), POOL_GROUP, POOL_GROUP), lambda b, i: (0, 0, 0)),
            full2(1, POOL_W),
            full2(CONV_WIDTH, CONV_W),
            full2(1, CONV_W),
            full2(1, CONV_W),
            full2(1, CONV_W),
        ],
        out_specs=[
            pl.BlockSpec((tl, D_MODEL), lambda b, i: (b * nt + i, 0)),
            pl.BlockSpec((None, POOL_BUF, POOL_W), lambda b, i: (b, 0, 0)),
            pl.BlockSpec((None, CONV_BUF, CONV_W), lambda b, i: (b, 0, 0)),
        ],
        out_shape=[
            jax.ShapeDtypeStruct((m, D_MODEL), BF16),
            jax.ShapeDtypeStruct((nb, POOL_BUF, POOL_W), F32),
            jax.ShapeDtypeStruct((nb, CONV_BUF, CONV_W), F32),
        ],
        scratch_shapes=[
            pltpu.VMEM((POOL_BUF + 1 + tl, POOL_W), F32),
            pltpu.VMEM((CONV_BUF + 2 + tl, CONV_W), F32),
            pltpu.VMEM((7, tl + 3 * 8, CONV_HEAD_DIM), F32),
        ],
        compiler_params=_params(40, 2),
        name="mixer_seq",
    )(z, pbuf, cbuf, wpool_bf, row(pscale), wconv, row(bconv), row(lng), row(lnb))


def _mixer_step_kernel(z_ref, pbuf_ref, cbuf_ref, wpool_ref, pscale_ref, wconv_ref, bconv_ref,
                       lng_ref, lnb_ref, mix_ref, u_ref, a_ref, *, start_pos):
    u_all = z_ref[:, 0:POOL_W]
    a_all = (z_ref[:, POOL_W:POOL_W + CONV_W]
             * jax.nn.sigmoid(z_ref[:, POOL_W + CONV_W:POOL_W + 2 * CONV_W]))
    u_ref[...] = u_all
    a_ref[...] = a_all

    for gi, w in enumerate(POOL_WINDOWS):
        sl = slice(gi * POOL_GROUP, (gi + 1) * POOL_GROUP)
        u = u_all[:, sl]
        s = u
        for j in range(1, w):
            s = s + pbuf_ref[:, POOL_BUF - j, sl]
        cnt = min(float(w), float(start_pos) + 1.0)
        dlt = s / cnt - u
        y = jnp.dot(dlt.astype(BF16), wpool_ref[gi], preferred_element_type=F32)
        mix_ref[:, sl] = (y * pscale_ref[:, sl]).astype(BF16)

    for hi in range(CONV_HEADS):
        sl = slice(hi * CONV_HEAD_DIM, (hi + 1) * CONV_HEAD_DIM)
        acc = a_all[:, sl] * wconv_ref[CONV_BUF:CONV_BUF + 1, sl]
        for j in range(CONV_BUF):
            acc = acc + cbuf_ref[:, j, sl] * wconv_ref[j:j + 1, sl]
        y = _layernorm_silu(acc + bconv_ref[:, sl], lng_ref[:, sl], lnb_ref[:, sl])
        mix_ref[:, POOL_W + hi * CONV_HEAD_DIM:POOL_W + (hi + 1) * CONV_HEAD_DIM] = y.astype(BF16)


def _mixer_step(z, pbuf, cbuf, wpool_bf, pscale, wconv, bconv, lng, lnb, *, start_pos):
    m = z.shape[0]
    bt = 16
    row = lambda a: a.reshape(1, -1)
    full2 = lambda r, c: pl.BlockSpec((r, c), lambda b: (0, 0))
    kern = functools.partial(_mixer_step_kernel, start_pos=start_pos)
    return pl.pallas_call(
        kern,
        grid=(m // bt,),
        in_specs=[
            pl.BlockSpec((bt, z.shape[1]), lambda b: (b, 0)),
            pl.BlockSpec((bt, POOL_BUF, POOL_W), lambda b: (b, 0, 0)),
            pl.BlockSpec((bt, CONV_BUF, CONV_W), lambda b: (b, 0, 0)),
            pl.BlockSpec((len(POOL_WINDOWS), POOL_GROUP, POOL_GROUP), lambda b: (0, 0, 0)),
            full2(1, POOL_W),
            full2(CONV_WIDTH, CONV_W),
            full2(1, CONV_W),
            full2(1, CONV_W),
            full2(1, CONV_W),
        ],
        out_specs=[
            pl.BlockSpec((bt, D_MODEL), lambda b: (b, 0)),
            pl.BlockSpec((bt, POOL_W), lambda b: (b, 0)),
            pl.BlockSpec((bt, CONV_W), lambda b: (b, 0)),
        ],
        out_shape=[
            jax.ShapeDtypeStruct((m, D_MODEL), BF16),
            jax.ShapeDtypeStruct((m, POOL_W), F32),
            jax.ShapeDtypeStruct((m, CONV_W), F32),
        ],
        compiler_params=_params(40, 1),
        name="mixer_step",
    )(z, pbuf, cbuf, wpool_bf, row(pscale), wconv, row(bconv), row(lng), row(lnb))


def _out_proj_kernel(m_ref, w_ref, x_ref, ga_ref, o_ref):
    o_ref[...] = x_ref[...] + ga_ref[...] * jnp.dot(m_ref[...], w_ref[...],
                                                    preferred_element_type=F32)


def _out_proj(mix, w_bf, x, ga, *, tm, rows_per_seq, per_row):
    m, d = mix.shape
    n = w_bf.shape[1]
    tn = 1024
    return pl.pallas_call(
        _out_proj_kernel,
        grid=(m // tm, n // tn),
        in_specs=[
            pl.BlockSpec((tm, d), lambda i, j: (i, 0)),
            pl.BlockSpec((d, tn), lambda i, j: (0, j)),
            pl.BlockSpec((tm, tn), lambda i, j: (i, j)),
            _mod_spec(per_row, tm, tn, rows_per_seq, lambda j: j),
        ],
        out_specs=pl.BlockSpec((tm, tn), lambda i, j: (i, j)),
        out_shape=jax.ShapeDtypeStruct((m, n), F32),
        compiler_params=_params(52, 2),
        name="out_proj",
    )(mix, w_bf, x, ga)


def _route_kernel(x_ref, g_ref, sc_ref, sh_ref, wr_ref, br_ref, cin_ref, h2_any, h2_ref, meta_ref,
                  gate_ref, cout_ref, run_scr, *, tm):
    del h2_any
    step = pl.program_id(0)

    @pl.when(step == 0)
    def _():
        run_scr[...] = cin_ref[...]

    h2 = _rmsnorm(x_ref[...], g_ref[...]) * (1.0 + sc_ref[...]) + sh_ref[...]
    for c in range(ROW_CHUNKS):
        h2_ref[pl.ds(c, tm, stride=ROW_PITCH), :] = h2[:, c * 128:(c + 1) * 128]
    for c in range(ROW_CHUNKS, ROW_PITCH):
        h2_ref[pl.ds(c, tm, stride=ROW_PITCH), :] = jnp.zeros((tm, 128), F32)
    logits = jnp.dot(h2.astype(BF16), wr_ref[...], preferred_element_type=F32) + br_ref[...]

    e_iota = lax.broadcasted_iota(I32, (tm, N_EXPERTS), 1).astype(F32)
    work = logits
    vals, idxs, sels = [], [], []
    for _ in range(TOP_K):
        mx = jnp.max(work, axis=1, keepdims=True)
        idx = jnp.min(jnp.where(work == mx, e_iota, float(N_EXPERTS)), axis=1, keepdims=True)
        sel = e_iota == idx
        vals.append(mx)
        idxs.append(idx)
        sels.append(sel)
        work = jnp.where(sel, -jnp.inf, work)

    exps = [jnp.exp(v - vals[0]) for v in vals]
    den = exps[0] + exps[1] + exps[2] + exps[3]

    onehot = jnp.zeros((tm, N_EXPERTS), F32)
    for sel in sels:
        onehot = onehot + jnp.where(sel, 1.0, 0.0)
    r_iota = lax.broadcasted_iota(I32, (tm, tm), 0)
    c_iota = lax.broadcasted_iota(I32, (tm, tm), 1)
    lower = jnp.where(c_iota < r_iota, 1.0, 0.0).astype(BF16)
    before = jnp.dot(lower, onehot.astype(BF16), preferred_element_type=F32) + run_scr[...]

    lane = lax.broadcasted_iota(I32, (tm, 128), 1)
    meta = jnp.zeros((tm, 128), F32)
    gate = jnp.zeros((tm, 128), F32)
    for k in range(TOP_K):
        rank = jnp.sum(jnp.where(sels[k], before, 0.0), axis=1, keepdims=True)
        meta = jnp.where(lane == k, idxs[k], meta)
        meta = jnp.where(lane == TOP_K + k, rank, meta)
        gate = jnp.where(lane == k, exps[k] / den, gate)
    meta_ref[...] = meta.astype(I32)
    gate_ref[...] = gate

    run_scr[...] = run_scr[...] + jnp.sum(onehot, axis=0, keepdims=True)
    cout_ref[...] = run_scr[...]


def _route(x1, g, sc, sh, wr_bf, br, counts_in, h2_all, *, tm, rows_per_seq, per_row, row_block0):
    m, d = x1.shape
    zero = lambda j: 0
    mod = lambda: (pl.BlockSpec((tm, d), lambda i: (i, 0)) if per_row else
                   pl.BlockSpec((None, 1, d), lambda i: ((i * tm) // rows_per_seq, 0, 0)))
    kern = functools.partial(_route_kernel, tm=tm)
    return pl.pallas_call(
        kern,
        grid=(m // tm,),
        in_specs=[
            pl.BlockSpec((tm, d), lambda i: (i, 0)),
            pl.BlockSpec((1, d), lambda i: (0, 0)),
            mod(),
            mod(),
            pl.BlockSpec((d, N_EXPERTS), lambda i: (0, 0)),
            pl.BlockSpec((1, N_EXPERTS), lambda i: (0, 0)),
            pl.BlockSpec((1, N_EXPERTS), lambda i: (0, 0)),
            pl.BlockSpec(memory_space=pl.ANY),
        ],
        out_specs=[
            pl.BlockSpec((tm * ROW_PITCH, 128), lambda i: (row_block0 + i, 0)),
            pl.BlockSpec((tm, 128), lambda i: (i, 0)),
            pl.BlockSpec((tm, 128), lambda i: (i, 0)),
            pl.BlockSpec((1, N_EXPERTS), lambda i: (0, 0)),
        ],
        out_shape=[
            jax.ShapeDtypeStruct(h2_all.shape, F32),
            jax.ShapeDtypeStruct((m, 128), I32),
            jax.ShapeDtypeStruct((m, 128), F32),
            jax.ShapeDtypeStruct((1, N_EXPERTS), F32),
        ],
        scratch_shapes=[pltpu.VMEM((1, N_EXPERTS), F32)],
        input_output_aliases={7: 0},
        compiler_params=_params(52, 1),
        name="route",
    )(x1, g, sc, sh, wr_bf, br, counts_in, h2_all)


def _gather_kernel(valid_ref, omap_ref, nxt_ref, slot_ref, tok0_ref, tokn_ref, h2_hbm, o_ref, buf,
                   sem):
    del omap_ref
    s = pl.program_id(0)

    def issue(tok_ref, slot):
        def body(r, carry):
            src = h2_hbm.at[pl.ds(tok_ref[0, r], ROW_CHUNKS), :]
            dst = buf.at[slot, pl.ds(r * ROW_PITCH, ROW_CHUNKS), :]
            pltpu.make_async_copy(src, dst, sem.at[slot]).start()
            return carry

        lax.fori_loop(0, GATHER_ROWS, body, 0, unroll=8)

    @pl.when(s == 0)
    def _():
        issue(tok0_ref, 0)

    @pl.when(valid_ref[s] > 0)
    def _():
        slot = slot_ref[s]

        @pl.when(nxt_ref[s] >= 0)
        def _():
            issue(tokn_ref, 1 - slot)

        nrow = GATHER_ROWS * ROW_CHUNKS
        pltpu.make_async_copy(h2_hbm.at[pl.ds(0, nrow), :], buf.at[slot, pl.ds(0, nrow), :],
                              sem.at[slot]).wait()
        for c in range(ROW_CHUNKS):
            piece = buf[slot, pl.ds(c, GATHER_ROWS, stride=ROW_PITCH), :]
            o_ref[:, c * 128:(c + 1) * 128] = piece.astype(BF16)


def _gather(valid, slot_row, h2_all, n_steps):
    d = D_MODEL
    steps = jnp.arange(n_steps, dtype=I32)
    live = valid > 0
    omap = lax.cummax(jnp.where(live, steps, 0))
    later = jnp.where(live, steps, n_steps)
    nxt = lax.cummin(jnp.concatenate([later[1:], jnp.full((1,), n_steps, I32)]), reverse=True)
    nxt = jnp.where(nxt >= n_steps, -1, nxt).astype(I32)
    slot = ((jnp.cumsum(live.astype(I32)) - 1) % 2).astype(I32)
    toks = slot_row.reshape(n_steps, 1, GATHER_ROWS)
    return pl.pallas_call(
        _gather_kernel,
        grid_spec=pltpu.PrefetchScalarGridSpec(
            num_scalar_prefetch=4,
            grid=(n_steps,),
            in_specs=[
                pl.BlockSpec((None, 1, GATHER_ROWS), lambda s, v, om, nx, sl: (0, 0, 0),
                             memory_space=pltpu.SMEM),
                pl.BlockSpec((None, 1, GATHER_ROWS),
                             lambda s, v, om, nx, sl: (jnp.maximum(nx[s], 0), 0, 0),
                             memory_space=pltpu.SMEM),
                pl.BlockSpec(memory_space=pl.ANY),
            ],
            out_specs=pl.BlockSpec((GATHER_ROWS, d), lambda s, v, om, nx, sl: (om[s], 0)),
            scratch_shapes=[pltpu.VMEM((2, GATHER_ROWS * ROW_PITCH, 128), F32),
                            pltpu.SemaphoreType.DMA((2,))],
        ),
        out_shape=jax.ShapeDtypeStruct((n_steps * GATHER_ROWS, d), BF16),
        compiler_params=_params(32, 1),
        name="gather",
    )(valid, omap, nxt, slot, toks, toks, h2_all)


def _for_row_pieces(nsub, piece):
    big = 2 * MOE_SUB
    npair = nsub // 2

    def body(j, carry):
        piece(pl.multiple_of(j * big, big), big)
        return carry

    lax.fori_loop(0, npair, body, 0)

    @pl.when(nsub % 2 == 1)
    def _():
        piece(pl.multiple_of(npair * big, big), MOE_SUB)


def _moe_up_kernel(be_ref, nsub_ref, bmap_ref, xs_ref, wg_ref, wu_ref, bg_ref, bu_ref, act_ref,
                   w_scr):
    del be_ref, bmap_ref
    nsub = nsub_ref[pl.program_id(0)]
    tf = MOE_TF

    @pl.when(nsub > 0)
    def _():
        w_scr[:, 0:tf] = wg_ref[...].astype(BF16)
        w_scr[:, tf:2 * tf] = wu_ref[...].astype(BF16)

        def piece(row0, size):
            rows = pl.ds(row0, size)
            gu = jnp.dot(xs_ref[rows, :], w_scr[...], preferred_element_type=F32)
            g = jnp.minimum(gu[:, 0:tf] + bg_ref[...], SWIGLU_LIMIT)
            u = jnp.clip(gu[:, tf:2 * tf] + bu_ref[...], -SWIGLU_LIMIT, SWIGLU_LIMIT)
            act = (u + 1.0) * (g * jax.nn.sigmoid(SWIGLU_ALPHA * g))
            act_ref[rows, :] = act.astype(BF16)

        _for_row_pieces(nsub, piece)

        def fill(j, carry):
            rows = pl.ds(pl.multiple_of(j * MOE_SUB, MOE_SUB), MOE_SUB)
            act_ref[rows, :] = jnp.zeros((MOE_SUB, tf), BF16)
            return carry

        lax.fori_loop(nsub, MOE_NSUB, fill, 0)


def _moe_up(be, nsub, bmap, xs, w_gu, b_gu, n_blocks, n_live):
    d = xs.shape[1]
    tf = MOE_TF
    nf = D_FF // tf
    live = lambda b, ns: ns[b] > 0
    return pl.pallas_call(
        _moe_up_kernel,
        grid_spec=pltpu.PrefetchScalarGridSpec(
            num_scalar_prefetch=3,
            grid=(n_live, nf),
            in_specs=[
                pl.BlockSpec((MOE_TM, d), lambda b, f, be, ns, bm: (bm[b], 0)),
                pl.BlockSpec((None, d, tf),
                             lambda b, f, be, ns, bm: (be[b], 0, jnp.where(live(b, ns), f, nf - 1))),
                pl.BlockSpec((None, d, tf),
                             lambda b, f, be, ns, bm: (be[b], 0,
                                                       nf + jnp.where(live(b, ns), f, nf - 1))),
                pl.BlockSpec((None, 1, tf),
                             lambda b, f, be, ns, bm: (be[b], 0, jnp.where(live(b, ns), f, nf - 1))),
                pl.BlockSpec((None, 1, tf),
                             lambda b, f, be, ns, bm: (be[b], 0,
                                                       nf + jnp.where(live(b, ns), f, nf - 1))),
            ],
            out_specs=pl.BlockSpec(
                (MOE_TM, tf),
                lambda b, f, be, ns, bm: (jnp.where(live(b, ns), b, n_blocks),
                                          jnp.where(live(b, ns), f, 0))),
            scratch_shapes=[pltpu.VMEM((d, 2 * tf), BF16)],
        ),
        out_shape=jax.ShapeDtypeStruct(((n_blocks + 1) * MOE_TM, D_FF), BF16),
        compiler_params=_params(56, 2),
        name="moe_up",
    )(be, nsub, bmap, xs, w_gu, w_gu, b_gu, b_gu)


def _moe_down_kernel(be_ref, nsub_ref, bmap_ref, act_ref, wd_ref, bd_ref, ys_ref, w_scr):
    del be_ref, bmap_ref
    nsub = nsub_ref[pl.program_id(0)]

    @pl.when(nsub > 0)
    def _():
        w_scr[...] = wd_ref[...].astype(BF16)

        def piece(row0, size):
            rows = pl.ds(row0, size)
            y = jnp.dot(act_ref[rows, :], w_scr[...], preferred_element_type=F32)
            ys_ref[rows, :] = y + bd_ref[...]

        _for_row_pieces(nsub, piece)

        def fill(j, carry):
            rows = pl.ds(pl.multiple_of(j * MOE_SUB, MOE_SUB), MOE_SUB)
            ys_ref[rows, :] = jnp.zeros((MOE_SUB, MOE_TN), F32)
            return carry

        lax.fori_loop(nsub, MOE_NSUB, fill, 0)


def _moe_down(be, nsub, bmap, act, w_down, b_down, n_blocks, n_live):
    dff = act.shape[1]
    tn = MOE_TN
    nn = D_MODEL // tn
    live = lambda b, ns: ns[b] > 0
    return pl.pallas_call(
        _moe_down_kernel,
        grid_spec=pltpu.PrefetchScalarGridSpec(
            num_scalar_prefetch=3,
            grid=(n_live, nn),
            in_specs=[
                pl.BlockSpec((MOE_TM, dff), lambda b, n, be, ns, bm: (bm[b], 0)),
                pl.BlockSpec((None, dff, tn),
                             lambda b, n, be, ns, bm: (be[b], 0, jnp.where(live(b, ns), n, nn - 1))),
                pl.BlockSpec((None, 1, tn),
                             lambda b, n, be, ns, bm: (be[b], 0, jnp.where(live(b, ns), n, nn - 1))),
            ],
            out_specs=pl.BlockSpec(
                (MOE_TM, tn),
                lambda b, n, be, ns, bm: (jnp.where(live(b, ns), b, n_blocks),
                                          jnp.where(live(b, ns), n, 0))),
            scratch_shapes=[pltpu.VMEM((dff, tn), BF16)],
        ),
        out_shape=jax.ShapeDtypeStruct(((n_blocks + 1) * MOE_TM, D_MODEL), F32),
        compiler_params=_params(56, 2),
        name="moe_down",
    )(be, nsub, bmap, act, w_down, b_down)


def _combine_kernel(dest_ref, x_ref, gate_ref, ga_ref, gf_ref, ys_hbm, o_ref, buf, sem):
    def issue(r, carry):
        p = dest_ref[0, r]
        pltpu.make_async_copy(ys_hbm.at[pl.ds(p, 1), :], buf.at[pl.ds(r, 1), :], sem).start()
        return carry

    lax.fori_loop(0, GATHER_ROWS, issue, 0)
    pltpu.make_async_copy(ys_hbm.at[pl.ds(0, GATHER_ROWS), :], buf, sem).wait()

    acc = gate_ref[:, 0:1] * buf[0:COMBINE_TOK, :]
    for k in range(1, TOP_K):
        acc = acc + gate_ref[:, k:k + 1] * buf[k * COMBINE_TOK:(k + 1) * COMBINE_TOK, :]
    x2 = x_ref[...] + ga_ref[...] * acc
    o_ref[...] = _rmsnorm(x2, gf_ref[...])


def _combine(dest_tiles, x1, gates, ga, g_final, ys, *, rows_per_seq, per_row):
    m, d = x1.shape
    tm = COMBINE_TOK
    ga_spec = (pl.BlockSpec((tm, d), lambda i: (i, 0)) if per_row else
               pl.BlockSpec((None, 1, d), lambda i: ((i * tm) // rows_per_seq, 0, 0)))
    return pl.pallas_call(
        _combine_kernel,
        grid=(m // tm,),
        in_specs=[
            pl.BlockSpec((None, 1, GATHER_ROWS), lambda i: (i, 0, 0), memory_space=pltpu.SMEM),
            pl.BlockSpec((tm, d), lambda i: (i, 0)),
            pl.BlockSpec((tm, 128), lambda i: (i, 0)),
            ga_spec,
            pl.BlockSpec((1, d), lambda i: (0, 0)),
            pl.BlockSpec(memory_space=pl.ANY),
        ],
        out_specs=pl.BlockSpec((tm, d), lambda i: (i, 0)),
        out_shape=jax.ShapeDtypeStruct((m, d), F32),
        scratch_shapes=[pltpu.VMEM((GATHER_ROWS, d), F32), pltpu.SemaphoreType.DMA(())],
        compiler_params=_params(32, 1),
        name="combine",
    )(dest_tiles, x1, gates, ga, g_final.reshape(1, d), ys)


def _routing_tables(counts, idx, rank, n_blocks):
    nb = (counts + MOE_TM - 1) // MOE_TM
    cum_nb = jnp.cumsum(nb)
    blk_start = cum_nb - nb
    total = cum_nb[-1]
    dest = (blk_start * MOE_TM)[idx] + rank

    b_ids = jnp.arange(n_blocks, dtype=I32)
    bmap = jnp.minimum(b_ids, total - 1)
    be = jnp.minimum(jnp.sum(bmap[:, None] >= cum_nb[None, :], axis=1), N_EXPERTS - 1).astype(I32)
    rows = jnp.clip(counts[be] - (b_ids - blk_start[be]) * MOE_TM, 0, MOE_TM)
    rows = jnp.where(b_ids < total, rows, 0)
    nsub = ((rows + MOE_SUB - 1) // MOE_SUB).astype(I32)
    return dest, be, nsub, bmap.astype(I32), total.astype(I32)


def kernel(x_prompt, x_sample, state_pool, state_conv, c_prompt, c_sample, w_ada, b_ada, g_norm1,
           w_in, w_pool, pool_scale, w_conv, b_conv, ln_g, ln_b, w_out, g_norm2, w_router,
           b_router, w_gu, b_gu, w_down, b_down, g_final):
    assert w_ada.shape[0] == 1, "single-layer trunk"
    nbp, seq, d = x_prompt.shape
    nbs, dec_seq, _ = x_sample.shape
    assert dec_seq == 1
    past_len = 16384
    tp, ts = nbp * seq, nbs
    t_all = tp + ts

    n_c = nbp + nbs
    pad_c = (-n_c) % 16
    c_all = jnp.concatenate([c_prompt, c_sample, jnp.zeros((pad_c, d), F32)], axis=0)
    mod = _ada(c_all, w_ada[0], b_ada[0])
    mod_p = [mod[:nbp, k * d:(k + 1) * d].reshape(nbp, 1, d) for k in range(N_MOD)]
    mod_s = [mod[nbp:n_c, k * d:(k + 1) * d] for k in range(N_MOD)]

    w_in_bf = w_in[0].astype(BF16)
    w_out_bf = w_out[0].astype(BF16)
    w_pool_bf = w_pool[0].astype(BF16)
    wr_bf = w_router[0].astype(BF16)
    row = lambda a: a.reshape(1, -1)

    xp = x_prompt.reshape(tp, d)
    xs_tok = x_sample.reshape(ts, d)
    tm = 512

    z_p = _in_proj(xp, row(g_norm1[0]), mod_p[1], mod_p[0], w_in_bf,
                   tm=tm, rows_per_seq=seq, per_row=False)
    z_s = _in_proj(xs_tok, row(g_norm1[0]), mod_s[1], mod_s[0], w_in_bf,
                   tm=ts, rows_per_seq=1, per_row=True)
    mix_p, npool_p, nconv_p = _mixer_seq(
        z_p, jnp.zeros((nbp, POOL_BUF, POOL_W), F32), jnp.zeros((nbp, CONV_BUF, CONV_W), F32),
        w_pool_bf, pool_scale[0], w_conv[0], b_conv[0], ln_g[0], ln_b[0],
        seq_len=seq, start_pos=0)
    mix_s, u_s, a_s = _mixer_step(
        z_s, state_pool[0], state_conv[0], w_pool_bf, pool_scale[0], w_conv[0], b_conv[0],
        ln_g[0], ln_b[0], start_pos=past_len)
    x1_p = _out_proj(mix_p, w_out_bf, xp, mod_p[2], tm=tm, rows_per_seq=seq, per_row=False)
    x1_s = _out_proj(mix_s, w_out_bf, xs_tok, mod_s[2], tm=ts, rows_per_seq=1, per_row=True)

    h2_all = jnp.zeros((t_all * ROW_PITCH, 128), F32)
    counts0 = jnp.zeros((1, N_EXPERTS), F32)
    h2_all, meta_p, gate_p, counts1 = _route(
        x1_p, row(g_norm2[0]), mod_p[4], mod_p[3], wr_bf, row(b_router[0]), counts0, h2_all,
        tm=tm, rows_per_seq=seq, per_row=False, row_block0=0)
    h2_all, meta_s, gate_s, counts2 = _route(
        x1_s, row(g_norm2[0]), mod_s[4], mod_s[3], wr_bf, row(b_router[0]), counts1, h2_all,
        tm=ts, rows_per_seq=1, per_row=True, row_block0=tp // ts)

    n_assign = t_all * TOP_K
    n_blocks = n_assign // MOE_TM + N_EXPERTS
    counts = counts2[0].astype(I32)
    idx = jnp.concatenate([meta_p[:, 0:TOP_K], meta_s[:, 0:TOP_K]], axis=0)
    rank = jnp.concatenate([meta_p[:, TOP_K:2 * TOP_K], meta_s[:, TOP_K:2 * TOP_K]], axis=0)
    dest, be, nsub, bmap, n_live = _routing_tables(counts, idx, rank, n_blocks)

    n_gsteps = n_blocks * (MOE_TM // GATHER_ROWS)
    tok_rows = jnp.broadcast_to((jnp.arange(t_all, dtype=I32) * ROW_PITCH)[:, None],
                                (t_all, TOP_K))
    slot_row = jnp.zeros((n_gsteps * GATHER_ROWS,), I32).at[dest.reshape(-1)].set(
        tok_rows.reshape(-1))
    gstep_row0 = jnp.arange(MOE_TM // GATHER_ROWS, dtype=I32) * GATHER_ROWS
    gvalid = (gstep_row0[None, :] < nsub[:, None] * MOE_SUB).astype(I32).reshape(-1)

    xs = _gather(gvalid, slot_row, h2_all, n_gsteps)
    act = _moe_up(be, nsub, bmap, xs, w_gu[0], b_gu[0].reshape(N_EXPERTS, 1, 2 * D_FF), n_blocks,
                  n_live)
    ys = _moe_down(be, nsub, bmap, act, w_down[0], b_down[0].reshape(N_EXPERTS, 1, d), n_blocks,
                   n_live)

    def dest_tiles(dst):
        n = dst.shape[0] // COMBINE_TOK
        return dst.reshape(n, COMBINE_TOK, TOP_K).transpose(0, 2, 1).reshape(n, 1, GATHER_ROWS)

    y_p = _combine(dest_tiles(dest[:tp]), x1_p, gate_p, mod_p[5], g_final, ys,
                   rows_per_seq=seq, per_row=False)
    y_s = _combine(dest_tiles(dest[tp:]), x1_s, gate_s, mod_s[5], g_final, ys,
                   rows_per_seq=1, per_row=True)

    new_pool_s = jnp.concatenate([state_pool[0][:, 1:], u_s[:, None, :]], axis=1)
    new_conv_s = jnp.concatenate([state_conv[0][:, 1:], a_s[:, None, :]], axis=1)
    return (y_p.reshape(nbp, seq, d), y_s.reshape(nbs, 1, d), npool_p[None], nconv_p[None],
            new_pool_s[None], new_conv_s[None])
```

```python
import functools

import jax
import jax.numpy as jnp
from jax import lax
from jax.experimental import pallas as pl
from jax.experimental.pallas import tpu as pltpu

F32 = jnp.float32
BF16 = jnp.bfloat16
I32 = jnp.int32

D_MODEL = 4096
POOL_W = 2048
CONV_W = 2048
POOL_WINDOWS = (2, 4, 8, 16)
POOL_GROUP = 512
POOL_BUF = 15
CONV_WIDTH = 31
CONV_BUF = 30
CONV_HEADS = 4
CONV_HEAD_DIM = 512
N_EXPERTS = 32
TOP_K = 4
D_FF = 4096
SWIGLU_LIMIT = 7.0
SWIGLU_ALPHA = 1.702
N_MOD = 6
EPS = 1e-5

MIB = 1024 * 1024

MOE_TM = 1280
MOE_SUB = 128
MOE_NSUB = MOE_TM // MOE_SUB
MOE_TF = 256
MOE_TN = 512
GATHER_ROWS = 256
COMBINE_TOK = GATHER_ROWS // TOP_K
ROW_CHUNKS = D_MODEL // 128
ROW_PITCH = ROW_CHUNKS + 4


def _params(vmem_mib, n_axes):
    return pltpu.CompilerParams(
        dimension_semantics=("arbitrary",) * n_axes,
        vmem_limit_bytes=vmem_mib * MIB,
    )


def _ada_kernel(c_ref, w_ref, b_ref, o_ref):
    c = c_ref[...]
    s = (c * jax.nn.sigmoid(c)).astype(BF16)
    o_ref[...] = jnp.dot(s, w_ref[...].astype(BF16), preferred_element_type=F32) + b_ref[...]


def _ada(c_all, w_ada, b_ada):
    m, d = c_all.shape
    n = w_ada.shape[1]
    tn = 512
    return pl.pallas_call(
        _ada_kernel,
        grid=(n // tn,),
        in_specs=[
            pl.BlockSpec((m, d), lambda j: (0, 0)),
            pl.BlockSpec((d, tn), lambda j: (0, j)),
            pl.BlockSpec((1, tn), lambda j: (0, j)),
        ],
        out_specs=pl.BlockSpec((m, tn), lambda j: (0, j)),
        out_shape=jax.ShapeDtypeStruct((m, n), F32),
        compiler_params=_params(40, 1),
        name="ada",
    )(c_all, w_ada, b_ada.reshape(1, n))


def _rmsnorm(x, g):
    return (x * lax.rsqrt(jnp.mean(x * x, axis=-1, keepdims=True) + EPS)) * g


def _in_proj_kernel(x_ref, g_ref, sc_ref, sh_ref, w_ref, z_ref, h_scr):
    @pl.when(pl.program_id(1) == 0)
    def _():
        h = _rmsnorm(x_ref[...], g_ref[...]) * (1.0 + sc_ref[...]) + sh_ref[...]
        h_scr[...] = h.astype(BF16)

    z_ref[...] = jnp.dot(h_scr[...], w_ref[...], preferred_element_type=F32)


def _mod_spec(per_row, tm, tn, rows_per_seq, col_of):
    if per_row:
        return pl.BlockSpec((tm, tn), lambda i, j: (i, col_of(j)))
    return pl.BlockSpec((None, 1, tn), lambda i, j: ((i * tm) // rows_per_seq, 0, col_of(j)))


def _in_proj(x, g, sc, sh, w_bf, *, tm, rows_per_seq, per_row):
    m, d = x.shape
    n = w_bf.shape[1]
    tn = 1024
    zero = lambda j: 0
    return pl.pallas_call(
        _in_proj_kernel,
        grid=(m // tm, n // tn),
        in_specs=[
            pl.BlockSpec((tm, d), lambda i, j: (i, 0)),
            pl.BlockSpec((1, d), lambda i, j: (0, 0)),
            _mod_spec(per_row, tm, d, rows_per_seq, zero),
            _mod_spec(per_row, tm, d, rows_per_seq, zero),
            pl.BlockSpec((d, tn), lambda i, j: (0, j)),
        ],
        out_specs=pl.BlockSpec((tm, tn), lambda i, j: (i, j)),
        out_shape=jax.ShapeDtypeStruct((m, n), F32),
        scratch_shapes=[pltpu.VMEM((tm, d), BF16)],
        compiler_params=_params(58, 2),
        name="in_proj",
    )(x, g, sc, sh, w_bf)


def _pool_counts(pos, w):
    return jnp.minimum(jnp.float32(w), pos + 1.0)


def _layernorm_silu(y, g, b):
    mu = jnp.mean(y, axis=-1, keepdims=True)
    dlt = y - mu
    var = jnp.mean(dlt * dlt, axis=-1, keepdims=True)
    yn = dlt * lax.rsqrt(var + EPS) * g + b
    return yn * jax.nn.sigmoid(yn)


def _mixer_seq_kernel(z_ref, pbuf_ref, cbuf_ref, wpool_ref, pscale_ref, wconv_ref, bconv_ref,
                      lng_ref, lnb_ref, mix_ref, npool_ref, nconv_ref, extu, exta, shf, *,
                      tl, start_pos):
    i = pl.program_id(1)
    hu, ha = POOL_BUF + 1, CONV_BUF + 2

    @pl.when(i == 0)
    def _():
        extu[0:1, :] = jnp.zeros((1, POOL_W), F32)
        exta[0:2, :] = jnp.zeros((2, CONV_W), F32)
        extu[1:hu, :] = pbuf_ref[...]
        exta[2:ha, :] = cbuf_ref[...]

    @pl.when(i > 0)
    def _():
        extu[0:hu, :] = extu[tl:tl + hu, :]
        exta[0:ha, :] = exta[tl:tl + ha, :]

    extu[hu:hu + tl, :] = z_ref[:, 0:POOL_W]
    exta[ha:ha + tl, :] = (z_ref[:, POOL_W:POOL_W + CONV_W]
                           * jax.nn.sigmoid(z_ref[:, POOL_W + CONV_W:POOL_W + 2 * CONV_W]))

    pos = (lax.broadcasted_iota(I32, (tl, 1), 0) + (i * tl + start_pos)).astype(F32)
    for gi, w in enumerate(POOL_WINDOWS):
        sl = slice(gi * POOL_GROUP, (gi + 1) * POOL_GROUP)
        u = extu[hu:hu + tl, sl]
        s = u
        for j in range(1, w):
            s = s + extu[hu - j:hu - j + tl, sl]
        dlt = s * (1.0 / _pool_counts(pos, w)) - u
        y = jnp.dot(dlt.astype(BF16), wpool_ref[gi], preferred_element_type=F32)
        mix_ref[:, sl] = (y * pscale_ref[:, sl]).astype(BF16)

    ncopy = tl + 3 * 8
    for hi in range(CONV_HEADS):
        sl = slice(hi * CONV_HEAD_DIM, (hi + 1) * CONV_HEAD_DIM)
        for b in range(1, 8):
            shf[b - 1, 0:ncopy, :] = exta[b:b + ncopy, sl]
        acc = None
        for j in range(CONV_WIDTH):
            a, b = divmod(2 + j, 8)
            if b == 0:
                src = exta[8 * a:8 * a + tl, sl]
            else:
                src = shf[b - 1, 8 * a:8 * a + tl, :]
            term = src * wconv_ref[j:j + 1, sl]
            acc = term if acc is None else acc + term
        y = _layernorm_silu(acc + bconv_ref[:, sl], lng_ref[:, sl], lnb_ref[:, sl])
        mix_ref[:, POOL_W + hi * CONV_HEAD_DIM:POOL_W + (hi + 1) * CONV_HEAD_DIM] = y.astype(BF16)

    @pl.when(i == pl.num_programs(1) - 1)
    def _():
        npool_ref[...] = extu[tl + 1:tl + hu, :]
        nconv_ref[...] = exta[tl + 2:tl + ha, :]


def _mixer_seq(z, pbuf, cbuf, wpool_bf, pscale, wconv, bconv, lng, lnb, *, seq_len, start_pos):
    m = z.shape[0]
    nb = m // seq_len
    tl = 256
    nt = seq_len // tl
    row = lambda a: a.reshape(1, -1)
    full2 = lambda r, c: pl.BlockSpec((r, c), lambda b, i: (0, 0))
    kern = functools.partial(_mixer_seq_kernel, tl=tl, start_pos=start_pos)
    return pl.pallas_call(
        kern,
        grid=(nb, nt),
        in_specs=[
            pl.BlockSpec((tl, z.shape[1]), lambda b, i: (b * nt + i, 0)),
            pl.BlockSpec((None, POOL_BUF, POOL_W), lambda b, i: (b, 0, 0)),
            pl.BlockSpec((None, CONV_BUF, CONV_W), lambda b, i: (b, 0, 0)),
            pl.BlockSpec((len(POOL_WINDOWS), POOL_GROUP, POOL_GROUP), lambda b, i: (0, 0, 0)),
            full2(1, POOL_W),
            full2(CONV_WIDTH, CONV_W),
            full2(1, CONV_W),
            full2(1, CONV_W),
            full2(1, CONV_W),
        ],
        out_specs=[
            pl.BlockSpec((tl, D_MODEL), lambda b, i: (b * nt + i, 0)),
            pl.BlockSpec((None, POOL_BUF, POOL_W), lambda b, i: (b, 0, 0)),
            pl.BlockSpec((None, CONV_BUF, CONV_W), lambda b, i: (b, 0, 0)),
        ],
        out_shape=[
            jax.ShapeDtypeStruct((m, D_MODEL), BF16),
            jax.ShapeDtypeStruct((nb, POOL_BUF, POOL_W), F32),
            jax.ShapeDtypeStruct((nb, CONV_BUF, CONV_W), F32),
        ],
        scratch_shapes=[
            pltpu.VMEM((POOL_BUF + 1 + tl, POOL_W), F32),
            pltpu.VMEM((CONV_BUF + 2 + tl, CONV_W), F32),
            pltpu.VMEM((7, tl + 3 * 8, CONV_HEAD_DIM), F32),
        ],
        compiler_params=_params(40, 2),
        name="mixer_seq",
    )(z, pbuf, cbuf, wpool_bf, row(pscale), wconv, row(bconv), row(lng), row(lnb))


def _mixer_step_kernel(z_ref, pbuf_ref, cbuf_ref, wpool_ref, pscale_ref, wconv_ref, bconv_ref,
                       lng_ref, lnb_ref, mix_ref, u_ref, a_ref, *, start_pos):
    u_all = z_ref[:, 0:POOL_W]
    a_all = (z_ref[:, POOL_W:POOL_W + CONV_W]
             * jax.nn.sigmoid(z_ref[:, POOL_W + CONV_W:POOL_W + 2 * CONV_W]))
    u_ref[...] = u_all
    a_ref[...] = a_all

    for gi, w in enumerate(POOL_WINDOWS):
        sl = slice(gi * POOL_GROUP, (gi + 1) * POOL_GROUP)
        u = u_all[:, sl]
        s = u
        for j in range(1, w):
            s = s + pbuf_ref[:, POOL_BUF - j, sl]
        cnt = min(float(w), float(start_pos) + 1.0)
        dlt = s / cnt - u
        y = jnp.dot(dlt.astype(BF16), wpool_ref[gi], preferred_element_type=F32)
        mix_ref[:, sl] = (y * pscale_ref[:, sl]).astype(BF16)

    for hi in range(CONV_HEADS):
        sl = slice(hi * CONV_HEAD_DIM, (hi + 1) * CONV_HEAD_DIM)
        acc = a_all[:, sl] * wconv_ref[CONV_BUF:CONV_BUF + 1, sl]
        for j in range(CONV_BUF):
            acc = acc + cbuf_ref[:, j, sl] * wconv_ref[j:j + 1, sl]
        y = _layernorm_silu(acc + bconv_ref[:, sl], lng_ref[:, sl], lnb_ref[:, sl])
        mix_ref[:, POOL_W + hi * CONV_HEAD_DIM:POOL_W + (hi + 1) * CONV_HEAD_DIM] = y.astype(BF16)


def _mixer_step(z, pbuf, cbuf, wpool_bf, pscale, wconv, bconv, lng, lnb, *, start_pos):
    m = z.shape[0]
    bt = 16
    row = lambda a: a.reshape(1, -1)
    full2 = lambda r, c: pl.BlockSpec((r, c), lambda b: (0, 0))
    kern = functools.partial(_mixer_step_kernel, start_pos=start_pos)
    return pl.pallas_call(
        kern,
        grid=(m // bt,),
        in_specs=[
            pl.BlockSpec((bt, z.shape[1]), lambda b: (b, 0)),
            pl.BlockSpec((bt, POOL_BUF, POOL_W), lambda b: (b, 0, 0)),
            pl.BlockSpec((bt, CONV_BUF, CONV_W), lambda b: (b, 0, 0)),
            pl.BlockSpec((len(POOL_WINDOWS), POOL_GROUP, POOL_GROUP), lambda b: (0, 0, 0)),
            full2(1, POOL_W),
            full2(CONV_WIDTH, CONV_W),
            full2(1, CONV_W),
            full2(1, CONV_W),
            full2(1, CONV_W),
        ],
        out_specs=[
            pl.BlockSpec((bt, D_MODEL), lambda b: (b, 0)),
            pl.BlockSpec((bt, POOL_W), lambda b: (b, 0)),
            pl.BlockSpec((bt, CONV_W), lambda b: (b, 0)),
        ],
        out_shape=[
            jax.ShapeDtypeStruct((m, D_MODEL), BF16),
            jax.ShapeDtypeStruct((m, POOL_W), F32),
            jax.ShapeDtypeStruct((m, CONV_W), F32),
        ],
        compiler_params=_params(40, 1),
        name="mixer_step",
    )(z, pbuf, cbuf, wpool_bf, row(pscale), wconv, row(bconv), row(lng), row(lnb))


def _out_proj_kernel(m_ref, w_ref, x_ref, ga_ref, o_ref):
    o_ref[...] = x_ref[...] + ga_ref[...] * jnp.dot(m_ref[...], w_ref[...],
                                                    preferred_element_type=F32)


def _out_proj(mix, w_bf, x, ga, *, tm, rows_per_seq, per_row):
    m, d = mix.shape
    n = w_bf.shape[1]
    tn = 1024
    return pl.pallas_call(
        _out_proj_kernel,
        grid=(m // tm, n // tn),
        in_specs=[
            pl.BlockSpec((tm, d), lambda i, j: (i, 0)),
            pl.BlockSpec((d, tn), lambda i, j: (0, j)),
            pl.BlockSpec((tm, tn), lambda i, j: (i, j)),
            _mod_spec(per_row, tm, tn, rows_per_seq, lambda j: j),
        ],
        out_specs=pl.BlockSpec((tm, tn), lambda i, j: (i, j)),
        out_shape=jax.ShapeDtypeStruct((m, n), F32),
        compiler_params=_params(52, 2),
        name="out_proj",
    )(mix, w_bf, x, ga)


def _route_kernel(x_ref, g_ref, sc_ref, sh_ref, wr_ref, br_ref, cin_ref, h2_any, h2_ref, meta_ref,
                  gate_ref, cout_ref, run_scr, *, tm):
    del h2_any
    step = pl.program_id(0)

    @pl.when(step == 0)
    def _():
        run_scr[...] = cin_ref[...]

    h2 = _rmsnorm(x_ref[...], g_ref[...]) * (1.0 + sc_ref[...]) + sh_ref[...]
    for c in range(ROW_CHUNKS):
        h2_ref[pl.ds(c, tm, stride=ROW_PITCH), :] = h2[:, c * 128:(c + 1) * 128]
    for c in range(ROW_CHUNKS, ROW_PITCH):
        h2_ref[pl.ds(c, tm, stride=ROW_PITCH), :] = jnp.zeros((tm, 128), F32)
    logits = jnp.dot(h2.astype(BF16), wr_ref[...], preferred_element_type=F32) + br_ref[...]

    e_iota = lax.broadcasted_iota(I32, (tm, N_EXPERTS), 1).astype(F32)
    work = logits
    vals, idxs, sels = [], [], []
    for _ in range(TOP_K):
        mx = jnp.max(work, axis=1, keepdims=True)
        idx = jnp.min(jnp.where(work == mx, e_iota, float(N_EXPERTS)), axis=1, keepdims=True)
        sel = e_iota == idx
        vals.append(mx)
        idxs.append(idx)
        sels.append(sel)
        work = jnp.where(sel, -jnp.inf, work)

    exps = [jnp.exp(v - vals[0]) for v in vals]
    den = exps[0] + exps[1] + exps[2] + exps[3]

    onehot = jnp.zeros((tm, N_EXPERTS), F32)
    for sel in sels:
        onehot = onehot + jnp.where(sel, 1.0, 0.0)
    r_iota = lax.broadcasted_iota(I32, (tm, tm), 0)
    c_iota = lax.broadcasted_iota(I32, (tm, tm), 1)
    lower = jnp.where(c_iota < r_iota, 1.0, 0.0).astype(BF16)
    before = jnp.dot(lower, onehot.astype(BF16), preferred_element_type=F32) + run_scr[...]

    lane = lax.broadcasted_iota(I32, (tm, 128), 1)
    meta = jnp.zeros((tm, 128), F32)
    gate = jnp.zeros((tm, 128), F32)
    for k in range(TOP_K):
        rank = jnp.sum(jnp.where(sels[k], before, 0.0), axis=1, keepdims=True)
        meta = jnp.where(lane == k, idxs[k], meta)
        meta = jnp.where(lane == TOP_K + k, rank, meta)
        gate = jnp.where(lane == k, exps[k] / den, gate)
    meta_ref[...] = meta.astype(I32)
    gate_ref[...] = gate

    run_scr[...] = run_scr[...] + jnp.sum(onehot, axis=0, keepdims=True)
    cout_ref[...] = run_scr[...]


def _route(x1, g, sc, sh, wr_bf, br, counts_in, h2_all, *, tm, rows_per_seq, per_row, row_block0):
    m, d = x1.shape
    zero = lambda j: 0
    mod = lambda: (pl.BlockSpec((tm, d), lambda i: (i, 0)) if per_row else
                   pl.BlockSpec((None, 1, d), lambda i: ((i * tm) // rows_per_seq, 0, 0)))
    kern = functools.partial(_route_kernel, tm=tm)
    return pl.pallas_call(
        kern,
        grid=(m // tm,),
        in_specs=[
            pl.BlockSpec((tm, d), lambda i: (i, 0)),
            pl.BlockSpec((1, d), lambda i: (0, 0)),
            mod(),
            mod(),
            pl.BlockSpec((d, N_EXPERTS), lambda i: (0, 0)),
            pl.BlockSpec((1, N_EXPERTS), lambda i: (0, 0)),
            pl.BlockSpec((1, N_EXPERTS), lambda i: (0, 0)),
            pl.BlockSpec(memory_space=pl.ANY),
        ],
        out_specs=[
            pl.BlockSpec((tm * ROW_PITCH, 128), lambda i: (row_block0 + i, 0)),
            pl.BlockSpec((tm, 128), lambda i: (i, 0)),
            pl.BlockSpec((tm, 128), lambda i: (i, 0)),
            pl.BlockSpec((1, N_EXPERTS), lambda i: (0, 0)),
        ],
        out_shape=[
            jax.ShapeDtypeStruct(h2_all.shape, F32),
            jax.ShapeDtypeStruct((m, 128), I32),
            jax.ShapeDtypeStruct((m, 128), F32),
            jax.ShapeDtypeStruct((1, N_EXPERTS), F32),
        ],
        scratch_shapes=[pltpu.VMEM((1, N_EXPERTS), F32)],
        input_output_aliases={7: 0},
        compiler_params=_params(52, 1),
        name="route",
    )(x1, g, sc, sh, wr_bf, br, counts_in, h2_all)


def _gather_kernel(valid_ref, omap_ref, nxt_ref, slot_ref, tok0_ref, tokn_ref, h2_hbm, o_ref, buf,
                   sem):
    del omap_ref
    s = pl.program_id(0)

    def issue(tok_ref, slot):
        def body(r8, carry):
            for u in range(8):
                r = r8 * 8 + u
                src = h2_hbm.at[pl.ds(tok_ref[0, r], ROW_CHUNKS), :]
                dst = buf.at[slot, pl.ds(r * ROW_PITCH, ROW_CHUNKS), :]
                pltpu.make_async_copy(src, dst, sem.at[slot]).start(priority=u % 2)
            return carry

        lax.fori_loop(0, GATHER_ROWS // 8, body, 0)

    @pl.when(s == 0)
    def _():
        issue(tok0_ref, 0)

    @pl.when(valid_ref[s] > 0)
    def _():
        slot = slot_ref[s]

        @pl.when(nxt_ref[s] >= 0)
        def _():
            issue(tokn_ref, 1 - slot)

        nrow = GATHER_ROWS * ROW_CHUNKS
        pltpu.make_async_copy(h2_hbm.at[pl.ds(0, nrow), :], buf.at[slot, pl.ds(0, nrow), :],
                              sem.at[slot]).wait()
        for c in range(ROW_CHUNKS):
            piece = buf[slot, pl.ds(c, GATHER_ROWS, stride=ROW_PITCH), :]
            o_ref[:, c * 128:(c + 1) * 128] = piece.astype(BF16)


def _gather(valid, slot_row, h2_all, n_steps):
    d = D_MODEL
    steps = jnp.arange(n_steps, dtype=I32)
    live = valid > 0
    omap = lax.cummax(jnp.where(live, steps, 0))
    later = jnp.where(live, steps, n_steps)
    nxt = lax.cummin(jnp.concatenate([later[1:], jnp.full((1,), n_steps, I32)]), reverse=True)
    nxt = jnp.where(nxt >= n_steps, -1, nxt).astype(I32)
    slot = ((jnp.cumsum(live.astype(I32)) - 1) % 2).astype(I32)
    toks = slot_row.reshape(n_steps, 1, GATHER_ROWS)
    return pl.pallas_call(
        _gather_kernel,
        grid_spec=pltpu.PrefetchScalarGridSpec(
            num_scalar_prefetch=4,
            grid=(n_steps,),
            in_specs=[
                pl.BlockSpec((None, 1, GATHER_ROWS), lambda s, v, om, nx, sl: (0, 0, 0),
                             memory_space=pltpu.SMEM),
                pl.BlockSpec((None, 1, GATHER_ROWS),
                             lambda s, v, om, nx, sl: (jnp.maximum(nx[s], 0), 0, 0),
                             memory_space=pltpu.SMEM),
                pl.BlockSpec(memory_space=pl.ANY),
            ],
            out_specs=pl.BlockSpec((GATHER_ROWS, d), lambda s, v, om, nx, sl: (om[s], 0)),
            scratch_shapes=[pltpu.VMEM((2, GATHER_ROWS * ROW_PITCH, 128), F32),
                            pltpu.SemaphoreType.DMA((2,))],
        ),
        out_shape=jax.ShapeDtypeStruct((n_steps * GATHER_ROWS, d), BF16),
        compiler_params=_params(32, 1),
        name="gather",
    )(valid, omap, nxt, slot, toks, toks, h2_all)


def _for_row_pieces(nsub, piece, prologue):
    top = 1
    while top * 2 <= MOE_NSUB:
        top *= 2
    has_top = (nsub & top) != 0

    @pl.when(has_top)
    def _():
        prologue()
        piece(0, top * MOE_SUB)

    @pl.when(jnp.logical_not(has_top))
    def _():
        prologue()

    units = top // 2
    while units >= 1:
        @pl.when((nsub & units) != 0)
        def _(units=units):
            start = (nsub & (-2 * units)) * MOE_SUB
            piece(pl.multiple_of(start, units * MOE_SUB), units * MOE_SUB)

        units //= 2


def _moe_up_kernel(be_ref, nsub_ref, bmap_ref, xs_ref, wg_ref, wu_ref, bg_ref, bu_ref, act_ref,
                   w_scr):
    del be_ref, bmap_ref
    nsub = nsub_ref[pl.program_id(0)]
    tf = MOE_TF

    @pl.when(nsub > 0)
    def _():
        def cast_weights():
            w_scr[:, 0:tf] = wg_ref[...].astype(BF16)
            w_scr[:, tf:2 * tf] = wu_ref[...].astype(BF16)

        def piece(row0, size):
            rows = pl.ds(row0, size)
            gu = jnp.dot(xs_ref[rows, :], w_scr[...], preferred_element_type=F32)
            g = jnp.minimum(gu[:, 0:tf] + bg_ref[...], SWIGLU_LIMIT)
            u = jnp.clip(gu[:, tf:2 * tf] + bu_ref[...], -SWIGLU_LIMIT, SWIGLU_LIMIT)
            act = (u + 1.0) * (g * jax.nn.sigmoid(SWIGLU_ALPHA * g))
            act_ref[rows, :] = act.astype(BF16)

        _for_row_pieces(nsub, piece, cast_weights)

        def fill(j, carry):
            rows = pl.ds(pl.multiple_of(j * MOE_SUB, MOE_SUB), MOE_SUB)
            act_ref[rows, :] = jnp.zeros((MOE_SUB, tf), BF16)
            return carry

        lax.fori_loop(nsub, MOE_NSUB, fill, 0)


def _moe_up(be, nsub, bmap, xs, w_gu, b_gu, n_blocks, n_live):
    d = xs.shape[1]
    tf = MOE_TF
    nf = D_FF // tf
    live = lambda b, ns: ns[b] > 0
    return pl.pallas_call(
        _moe_up_kernel,
        grid_spec=pltpu.PrefetchScalarGridSpec(
            num_scalar_prefetch=3,
            grid=(n_live, nf),
            in_specs=[
                pl.BlockSpec((MOE_TM, d), lambda b, f, be, ns, bm: (bm[b], 0)),
                pl.BlockSpec((None, d, tf),
                             lambda b, f, be, ns, bm: (be[b], 0, jnp.where(live(b, ns), f, nf - 1))),
                pl.BlockSpec((None, d, tf),
                             lambda b, f, be, ns, bm: (be[b], 0,
                                                       nf + jnp.where(live(b, ns), f, nf - 1))),
                pl.BlockSpec((None, 1, tf),
                             lambda b, f, be, ns, bm: (be[b], 0, jnp.where(live(b, ns), f, nf - 1))),
                pl.BlockSpec((None, 1, tf),
                             lambda b, f, be, ns, bm: (be[b], 0,
                                                       nf + jnp.where(live(b, ns), f, nf - 1))),
            ],
            out_specs=pl.BlockSpec(
                (MOE_TM, tf),
                lambda b, f, be, ns, bm: (jnp.where(live(b, ns), b, n_blocks),
                                          jnp.where(live(b, ns), f, 0))),
            scratch_shapes=[pltpu.VMEM((d, 2 * tf), BF16)],
        ),
        out_shape=jax.ShapeDtypeStruct(((n_blocks + 1) * MOE_TM, D_FF), BF16),
        compiler_params=_params(56, 2),
        name="moe_up",
    )(be, nsub, bmap, xs, w_gu, w_gu, b_gu, b_gu)


def _moe_down_kernel(be_ref, nsub_ref, bmap_ref, act_ref, wd_ref, bd_ref, ys_ref, w_scr):
    del be_ref, bmap_ref
    nsub = nsub_ref[pl.program_id(0)]

    @pl.when(nsub > 0)
    def _():
        def cast_weights():
            w_scr[...] = wd_ref[...].astype(BF16)

        def piece(row0, size):
            rows = pl.ds(row0, size)
            y = jnp.dot(act_ref[rows, :], w_scr[...], preferred_element_type=F32)
            ys_ref[rows, :] = y + bd_ref[...]

        _for_row_pieces(nsub, piece, cast_weights)

        def fill(j, carry):
            rows = pl.ds(pl.multiple_of(j * MOE_SUB, MOE_SUB), MOE_SUB)
            ys_ref[rows, :] = jnp.zeros((MOE_SUB, MOE_TN), F32)
            return carry

        lax.fori_loop(nsub, MOE_NSUB, fill, 0)


def _moe_down(be, nsub, bmap, act, w_down, b_down, n_blocks, n_live):
    dff = act.shape[1]
    tn = MOE_TN
    nn = D_MODEL // tn
    live = lambda b, ns: ns[b] > 0
    return pl.pallas_call(
        _moe_down_kernel,
        grid_spec=pltpu.PrefetchScalarGridSpec(
            num_scalar_prefetch=3,
            grid=(n_live, nn),
            in_specs=[
                pl.BlockSpec((MOE_TM, dff), lambda b, n, be, ns, bm: (bm[b], 0)),
                pl.BlockSpec((None, dff, tn),
                             lambda b, n, be, ns, bm: (be[b], 0, jnp.where(live(b, ns), n, nn - 1))),
                pl.BlockSpec((None, 1, tn),
                             lambda b, n, be, ns, bm: (be[b], 0, jnp.where(live(b, ns), n, nn - 1))),
            ],
            out_specs=pl.BlockSpec(
                (MOE_TM, tn),
                lambda b, n, be, ns, bm: (jnp.where(live(b, ns), b, n_blocks),
                                          jnp.where(live(b, ns), n, 0))),
            scratch_shapes=[pltpu.VMEM((dff, tn), BF16)],
        ),
        out_shape=jax.ShapeDtypeStruct(((n_blocks + 1) * MOE_TM, D_MODEL), F32),
        compiler_params=_params(56, 2),
        name="moe_down",
    )(be, nsub, bmap, act, w_down, b_down)


def _combine_kernel(dest0_ref, destn_ref, x_ref, gate_ref, ga_ref, gf_ref, ys_hbm, o_ref, buf, sem):
    i = pl.program_id(0)
    slot = i % 2

    def issue(dest_ref, slot):
        def body(r8, carry):
            for u in range(8):
                r = r8 * 8 + u
                src = ys_hbm.at[pl.ds(dest_ref[0, r], 1), :]
                dst = buf.at[slot, pl.ds(r, 1), :]
                pltpu.make_async_copy(src, dst, sem.at[slot]).start(priority=u % 2)
            return carry

        lax.fori_loop(0, GATHER_ROWS // 8, body, 0)

    @pl.when(i == 0)
    def _():
        issue(dest0_ref, 0)

    @pl.when(i + 1 < pl.num_programs(0))
    def _():
        issue(destn_ref, 1 - slot)

    pltpu.make_async_copy(ys_hbm.at[pl.ds(0, GATHER_ROWS), :], buf.at[slot], sem.at[slot]).wait()

    acc = gate_ref[:, 0:1] * buf[slot, 0:COMBINE_TOK, :]
    for k in range(1, TOP_K):
        acc = acc + gate_ref[:, k:k + 1] * buf[slot, k * COMBINE_TOK:(k + 1) * COMBINE_TOK, :]
    x2 = x_ref[...] + ga_ref[...] * acc
    o_ref[...] = _rmsnorm(x2, gf_ref[...])


def _combine(dest_tiles, x1, gates, ga, g_final, ys, *, rows_per_seq, per_row):
    m, d = x1.shape
    tm = COMBINE_TOK
    nt = m // tm
    ga_spec = (pl.BlockSpec((tm, d), lambda i: (i, 0)) if per_row else
               pl.BlockSpec((None, 1, d), lambda i: ((i * tm) // rows_per_seq, 0, 0)))
    return pl.pallas_call(
        _combine_kernel,
        grid=(nt,),
        in_specs=[
            pl.BlockSpec((None, 1, GATHER_ROWS), lambda i: (0, 0, 0), memory_space=pltpu.SMEM),
            pl.BlockSpec((None, 1, GATHER_ROWS), lambda i: (jnp.minimum(i + 1, nt - 1), 0, 0),
                         memory_space=pltpu.SMEM),
            pl.BlockSpec((tm, d), lambda i: (i, 0)),
            pl.BlockSpec((tm, 128), lambda i: (i, 0)),
            ga_spec,
            pl.BlockSpec((1, d), lambda i: (0, 0)),
            pl.BlockSpec(memory_space=pl.ANY),
        ],
        out_specs=pl.BlockSpec((tm, d), lambda i: (i, 0)),
        out_shape=jax.ShapeDtypeStruct((m, d), F32),
        scratch_shapes=[pltpu.VMEM((2, GATHER_ROWS, d), F32), pltpu.SemaphoreType.DMA((2,))],
        compiler_params=_params(32, 1),
        name="combine",
    )(dest_tiles, dest_tiles, x1, gates, ga, g_final.reshape(1, d), ys)


def _routing_tables(counts, idx, rank, n_blocks):
    nb = (counts + MOE_TM - 1) // MOE_TM
    cum_nb = jnp.cumsum(nb)
    blk_start = cum_nb - nb
    total = cum_nb[-1]
    dest = (blk_start * MOE_TM)[idx] + rank

    b_ids = jnp.arange(n_blocks, dtype=I32)
    bmap = jnp.maximum(jnp.minimum(b_ids, total - 1), 0)
    be = jnp.minimum(jnp.sum(bmap[:, None] >= cum_nb[None, :], axis=1), N_EXPERTS - 1).astype(I32)
    rows = jnp.clip(counts[be] - (b_ids - blk_start[be]) * MOE_TM, 0, MOE_TM)
    rows = jnp.where(b_ids < total, rows, 0)
    nsub = ((rows + MOE_SUB - 1) // MOE_SUB).astype(I32)
    return dest, be, nsub, bmap.astype(I32), total.astype(I32)


def kernel(x_prompt, x_sample, state_pool, state_conv, c_prompt, c_sample, w_ada, b_ada, g_norm1,
           w_in, w_pool, pool_scale, w_conv, b_conv, ln_g, ln_b, w_out, g_norm2, w_router,
           b_router, w_gu, b_gu, w_down, b_down, g_final):
    assert w_ada.shape[0] == 1, "single-layer trunk"
    nbp, seq, d = x_prompt.shape
    nbs, dec_seq, _ = x_sample.shape
    assert dec_seq == 1
    past_len = 16384
    tp, ts = nbp * seq, nbs
    t_all = tp + ts

    n_c = nbp + nbs
    pad_c = (-n_c) % 16
    c_all = jnp.concatenate([c_prompt, c_sample, jnp.zeros((pad_c, d), F32)], axis=0)
    mod = _ada(c_all, w_ada[0], b_ada[0])
    mod_p = [mod[:nbp, k * d:(k + 1) * d].reshape(nbp, 1, d) for k in range(N_MOD)]
    mod_s = [mod[nbp:n_c, k * d:(k + 1) * d] for k in range(N_MOD)]

    w_in_bf = w_in[0].astype(BF16)
    w_out_bf = w_out[0].astype(BF16)
    w_pool_bf = w_pool[0].astype(BF16)
    wr_bf = w_router[0].astype(BF16)
    row = lambda a: a.reshape(1, -1)

    xp = x_prompt.reshape(tp, d)
    xs_tok = x_sample.reshape(ts, d)
    tm = 512

    z_p = _in_proj(xp, row(g_norm1[0]), mod_p[1], mod_p[0], w_in_bf,
                   tm=tm, rows_per_seq=seq, per_row=False)
    z_s = _in_proj(xs_tok, row(g_norm1[0]), mod_s[1], mod_s[0], w_in_bf,
                   tm=ts, rows_per_seq=1, per_row=True)
    mix_p, npool_p, nconv_p = _mixer_seq(
        z_p, jnp.zeros((nbp, POOL_BUF, POOL_W), F32), jnp.zeros((nbp, CONV_BUF, CONV_W), F32),
        w_pool_bf, pool_scale[0], w_conv[0], b_conv[0], ln_g[0], ln_b[0],
        seq_len=seq, start_pos=0)
    mix_s, u_s, a_s = _mixer_step(
        z_s, state_pool[0], state_conv[0], w_pool_bf, pool_scale[0], w_conv[0], b_conv[0],
        ln_g[0], ln_b[0], start_pos=past_len)
    x1_p = _out_proj(mix_p, w_out_bf, xp, mod_p[2], tm=tm, rows_per_seq=seq, per_row=False)
    x1_s = _out_proj(mix_s, w_out_bf, xs_tok, mod_s[2], tm=ts, rows_per_seq=1, per_row=True)

    h2_all = jnp.zeros((t_all * ROW_PITCH, 128), F32)
    counts0 = jnp.zeros((1, N_EXPERTS), F32)
    h2_all, meta_p, gate_p, counts1 = _route(
        x1_p, row(g_norm2[0]), mod_p[4], mod_p[3], wr_bf, row(b_router[0]), counts0, h2_all,
        tm=tm, rows_per_seq=seq, per_row=False, row_block0=0)
    h2_all, meta_s, gate_s, counts2 = _route(
        x1_s, row(g_norm2[0]), mod_s[4], mod_s[3], wr_bf, row(b_router[0]), counts1, h2_all,
        tm=ts, rows_per_seq=1, per_row=True, row_block0=tp // ts)

    n_assign = t_all * TOP_K
    n_blocks = n_assign // MOE_TM + N_EXPERTS
    counts = counts2[0].astype(I32)
    idx = jnp.concatenate([meta_p[:, 0:TOP_K], meta_s[:, 0:TOP_K]], axis=0)
    rank = jnp.concatenate([meta_p[:, TOP_K:2 * TOP_K], meta_s[:, TOP_K:2 * TOP_K]], axis=0)
    dest, be, nsub, bmap, n_live = _routing_tables(counts, idx, rank, n_blocks)

    n_gsteps = n_blocks * (MOE_TM // GATHER_ROWS)
    tok_rows = jnp.broadcast_to((jnp.arange(t_all, dtype=I32) * ROW_PITCH)[:, None],
                                (t_all, TOP_K))
    slot_row = jnp.zeros((n_gsteps * GATHER_ROWS,), I32).at[dest.reshape(-1)].set(
        tok_rows.reshape(-1))
    gstep_row0 = jnp.arange(MOE_TM // GATHER_ROWS, dtype=I32) * GATHER_ROWS
    gvalid = (gstep_row0[None, :] < nsub[:, None] * MOE_SUB).astype(I32).reshape(-1)

    xs = _gather(gvalid, slot_row, h2_all, n_gsteps)
    act = _moe_up(be, nsub, bmap, xs, w_gu[0], b_gu[0].reshape(N_EXPERTS, 1, 2 * D_FF), n_blocks,
                  n_live)
    ys = _moe_down(be, nsub, bmap, act, w_down[0], b_down[0].reshape(N_EXPERTS, 1, d), n_blocks,
                   n_live)

    def dest_tiles(dst):
        n = dst.shape[0] // COMBINE_TOK
        return dst.reshape(n, COMBINE_TOK, TOP_K).transpose(0, 2, 1).reshape(n, 1, GATHER_ROWS)

    y_p = _combine(dest_tiles(dest[:tp]), x1_p, gate_p, mod_p[5], g_final, ys,
                   rows_per_seq=seq, per_row=False)
    y_s = _combine(dest_tiles(dest[tp:]), x1_s, gate_s, mod_s[5], g_final, ys,
                   rows_per_seq=1, per_row=True)

    new_pool_s = jnp.concatenate([state_pool[0][:, 1:], u_s[:, None, :]], axis=1)
    new_conv_s = jnp.concatenate([state_conv[0][:, 1:], a_s[:, None, :]], axis=1)
    return (y_p.reshape(nbp, seq, d), y_s.reshape(nbs, 1, d), npool_p[None], nconv_p[None],
            new_pool_s[None], new_conv_s[None])
```

```python
import functools

import jax
import jax.numpy as jnp
from jax import lax
from jax.experimental import pallas as pl
from jax.experimental.pallas import tpu as pltpu

F32 = jnp.float32
BF16 = jnp.bfloat16
I32 = jnp.int32

D_MODEL = 4096
POOL_W = 2048
CONV_W = 2048
POOL_WINDOWS = (2, 4, 8, 16)
POOL_GROUP = 512
POOL_BUF = 15
CONV_WIDTH = 31
CONV_BUF = 30
CONV_HEADS = 4
CONV_HEAD_DIM = 512
N_EXPERTS = 32
TOP_K = 4
D_FF = 4096
SWIGLU_LIMIT = 7.0
SWIGLU_ALPHA = 1.702
N_MOD = 6
EPS = 1e-5

MIB = 1024 * 1024

MOE_TM = 1280
MOE_SUB = 64
MOE_NSUB = MOE_TM // MOE_SUB
MOE_TF = 256
MOE_TN = 512
GATHER_ROWS = 256
COMBINE_TOK = GATHER_ROWS // TOP_K
ROW_CHUNKS = D_MODEL // 128
ROW_PITCH = ROW_CHUNKS + 4


def _params(vmem_mib, n_axes):
    return pltpu.CompilerParams(
        dimension_semantics=("arbitrary",) * n_axes,
        vmem_limit_bytes=vmem_mib * MIB,
    )


def _ada_kernel(c_ref, w_ref, b_ref, o_ref):
    c = c_ref[...]
    s = (c * jax.nn.sigmoid(c)).astype(BF16)
    o_ref[...] = jnp.dot(s, w_ref[...].astype(BF16), preferred_element_type=F32) + b_ref[...]


def _ada(c_all, w_ada, b_ada):
    m, d = c_all.shape
    n = w_ada.shape[1]
    tn = 512
    return pl.pallas_call(
        _ada_kernel,
        grid=(n // tn,),
        in_specs=[
            pl.BlockSpec((m, d), lambda j: (0, 0)),
            pl.BlockSpec((d, tn), lambda j: (0, j)),
            pl.BlockSpec((1, tn), lambda j: (0, j)),
        ],
        out_specs=pl.BlockSpec((m, tn), lambda j: (0, j)),
        out_shape=jax.ShapeDtypeStruct((m, n), F32),
        compiler_params=_params(40, 1),
        name="ada",
    )(c_all, w_ada, b_ada.reshape(1, n))


def _rmsnorm(x, g):
    return (x * lax.rsqrt(jnp.mean(x * x, axis=-1, keepdims=True) + EPS)) * g


def _in_proj_kernel(x_ref, g_ref, sc_ref, sh_ref, w_ref, z_ref, h_scr):
    def matmul():
        z_ref[...] = jnp.dot(h_scr[...], w_ref[...], preferred_element_type=F32)

    @pl.when(pl.program_id(1) == 0)
    def _():
        h = _rmsnorm(x_ref[...], g_ref[...]) * (1.0 + sc_ref[...]) + sh_ref[...]
        h_scr[...] = h.astype(BF16)
        matmul()

    @pl.when(pl.program_id(1) != 0)
    def _():
        matmul()


def _mod_spec(per_row, tm, tn, rows_per_seq, col_of):
    if per_row:
        return pl.BlockSpec((tm, tn), lambda i, j: (i, col_of(j)))
    return pl.BlockSpec((None, 1, tn), lambda i, j: ((i * tm) // rows_per_seq, 0, col_of(j)))


def _in_proj(x, g, sc, sh, w_bf, *, tm, rows_per_seq, per_row):
    m, d = x.shape
    n = w_bf.shape[1]
    tn = 1024
    zero = lambda j: 0
    return pl.pallas_call(
        _in_proj_kernel,
        grid=(m // tm, n // tn),
        in_specs=[
            pl.BlockSpec((tm, d), lambda i, j: (i, 0)),
            pl.BlockSpec((1, d), lambda i, j: (0, 0)),
            _mod_spec(per_row, tm, d, rows_per_seq, zero),
            _mod_spec(per_row, tm, d, rows_per_seq, zero),
            pl.BlockSpec((d, tn), lambda i, j: (0, j)),
        ],
        out_specs=pl.BlockSpec((tm, tn), lambda i, j: (i, j)),
        out_shape=jax.ShapeDtypeStruct((m, n), F32),
        scratch_shapes=[pltpu.VMEM((tm, d), BF16)],
        compiler_params=_params(58, 2),
        name="in_proj",
    )(x, g, sc, sh, w_bf)


def _pool_counts(pos, w):
    return jnp.minimum(jnp.float32(w), pos + 1.0)


def _layernorm_silu(y, g, b):
    mu = jnp.mean(y, axis=-1, keepdims=True)
    dlt = y - mu
    var = jnp.mean(dlt * dlt, axis=-1, keepdims=True)
    yn = dlt * lax.rsqrt(var + EPS) * g + b
    return yn * jax.nn.sigmoid(yn)


def _mixer_seq_kernel(z_ref, pbuf_ref, cbuf_ref, wpool_ref, pscale_ref, wconv_ref, bconv_ref,
                      lng_ref, lnb_ref, mix_ref, npool_ref, nconv_ref, extu, exta, shf, *,
                      tl, start_pos):
    i = pl.program_id(1)
    hu, ha = POOL_BUF + 1, CONV_BUF + 2

    @pl.when(i == 0)
    def _():
        extu[0:1, :] = jnp.zeros((1, POOL_W), F32)
        exta[0:2, :] = jnp.zeros((2, CONV_W), F32)
        extu[1:hu, :] = pbuf_ref[...]
        exta[2:ha, :] = cbuf_ref[...]

    @pl.when(i > 0)
    def _():
        extu[0:hu, :] = extu[tl:tl + hu, :]
        exta[0:ha, :] = exta[tl:tl + ha, :]

    extu[hu:hu + tl, :] = z_ref[:, 0:POOL_W]
    exta[ha:ha + tl, :] = (z_ref[:, POOL_W:POOL_W + CONV_W]
                           * jax.nn.sigmoid(z_ref[:, POOL_W + CONV_W:POOL_W + 2 * CONV_W]))

    pos = (lax.broadcasted_iota(I32, (tl, 1), 0) + (i * tl + start_pos)).astype(F32)
    for gi, w in enumerate(POOL_WINDOWS):
        sl = slice(gi * POOL_GROUP, (gi + 1) * POOL_GROUP)
        u = extu[hu:hu + tl, sl]
        s = u
        for j in range(1, w):
            s = s + extu[hu - j:hu - j + tl, sl]
        dlt = s * (1.0 / _pool_counts(pos, w)) - u
        y = jnp.dot(dlt.astype(BF16), wpool_ref[gi], preferred_element_type=F32)
        mix_ref[:, sl] = (y * pscale_ref[:, sl]).astype(BF16)

    ncopy = tl + 3 * 8
    for hi in range(CONV_HEADS):
        sl = slice(hi * CONV_HEAD_DIM, (hi + 1) * CONV_HEAD_DIM)
        for b in range(1, 8):
            shf[b - 1, 0:ncopy, :] = exta[b:b + ncopy, sl]
        acc = None
        for j in range(CONV_WIDTH):
            a, b = divmod(2 + j, 8)
            if b == 0:
                src = exta[8 * a:8 * a + tl, sl]
            else:
                src = shf[b - 1, 8 * a:8 * a + tl, :]
            term = src * wconv_ref[j:j + 1, sl]
            acc = term if acc is None else acc + term
        y = _layernorm_silu(acc + bconv_ref[:, sl], lng_ref[:, sl], lnb_ref[:, sl])
        mix_ref[:, POOL_W + hi * CONV_HEAD_DIM:POOL_W + (hi + 1) * CONV_HEAD_DIM] = y.astype(BF16)

    @pl.when(i == pl.num_programs(1) - 1)
    def _():
        npool_ref[...] = extu[tl + 1:tl + hu, :]
        nconv_ref[...] = exta[tl + 2:tl + ha, :]


def _mixer_seq(z, pbuf, cbuf, wpool_bf, pscale, wconv, bconv, lng, lnb, *, seq_len, start_pos):
    m = z.shape[0]
    nb = m // seq_len
    tl = 256
    nt = seq_len // tl
    row = lambda a: a.reshape(1, -1)
    full2 = lambda r, c: pl.BlockSpec((r, c), lambda b, i: (0, 0))
    kern = functools.partial(_mixer_seq_kernel, tl=tl, start_pos=start_pos)
    return pl.pallas_call(
        kern,
        grid=(nb, nt),
        in_specs=[
            pl.BlockSpec((tl, z.shape[1]), lambda b, i: (b * nt + i, 0)),
            pl.BlockSpec((None, POOL_BUF, POOL_W), lambda b, i: (b, 0, 0)),
            pl.BlockSpec((None, CONV_BUF, CONV_W), lambda b, i: (b, 0, 0)),
            pl.BlockSpec((len(POOL_WINDOWS), POOL_GROUP, POOL_GROUP), lambda b, i: (0, 0, 0)),
            full2(1, POOL_W),
            full2(CONV_WIDTH, CONV_W),
            full2(1, CONV_W),
            full2(1, CONV_W),
            full2(1, CONV_W),
        ],
        out_specs=[
            pl.BlockSpec((tl, D_MODEL), lambda b, i: (b * nt + i, 0)),
            pl.BlockSpec((None, POOL_BUF, POOL_W), lambda b, i: (b, 0, 0)),
            pl.BlockSpec((None, CONV_BUF, CONV_W), lambda b, i: (b, 0, 0)),
        ],
        out_shape=[
            jax.ShapeDtypeStruct((m, D_MODEL), BF16),
            jax.ShapeDtypeStruct((nb, POOL_BUF, POOL_W), F32),
            jax.ShapeDtypeStruct((nb, CONV_BUF, CONV_W), F32),
        ],
        scratch_shapes=[
            pltpu.VMEM((POOL_BUF + 1 + tl, POOL_W), F32),
            pltpu.VMEM((CONV_BUF + 2 + tl, CONV_W), F32),
            pltpu.VMEM((7, tl + 3 * 8, CONV_HEAD_DIM), F32),
        ],
        compiler_params=_params(40, 2),
        name="mixer_seq",
    )(z, pbuf, cbuf, wpool_bf, row(pscale), wconv, row(bconv), row(lng), row(lnb))


def _mixer_step_kernel(z_ref, pbuf_ref, cbuf_ref, wpool_ref, pscale_ref, wconv_ref, bconv_ref,
                       lng_ref, lnb_ref, mix_ref, u_ref, a_ref, *, start_pos):
    u_all = z_ref[:, 0:POOL_W]
    a_all = (z_ref[:, POOL_W:POOL_W + CONV_W]
             * jax.nn.sigmoid(z_ref[:, POOL_W + CONV_W:POOL_W + 2 * CONV_W]))
    u_ref[...] = u_all
    a_ref[...] = a_all

    for gi, w in enumerate(POOL_WINDOWS):
        sl = slice(gi * POOL_GROUP, (gi + 1) * POOL_GROUP)
        u = u_all[:, sl]
        s = u
        for j in range(1, w):
            s = s + pbuf_ref[:, POOL_BUF - j, sl]
        cnt = min(float(w), float(start_pos) + 1.0)
        dlt = s / cnt - u
        y = jnp.dot(dlt.astype(BF16), wpool_ref[gi], preferred_element_type=F32)
        mix_ref[:, sl] = (y * pscale_ref[:, sl]).astype(BF16)

    for hi in range(CONV_HEADS):
        sl = slice(hi * CONV_HEAD_DIM, (hi + 1) * CONV_HEAD_DIM)
        acc = a_all[:, sl] * wconv_ref[CONV_BUF:CONV_BUF + 1, sl]
        for j in range(CONV_BUF):
            acc = acc + cbuf_ref[:, j, sl] * wconv_ref[j:j + 1, sl]
        y = _layernorm_silu(acc + bconv_ref[:, sl], lng_ref[:, sl], lnb_ref[:, sl])
        mix_ref[:, POOL_W + hi * CONV_HEAD_DIM:POOL_W + (hi + 1) * CONV_HEAD_DIM] = y.astype(BF16)


def _mixer_step(z, pbuf, cbuf, wpool_bf, pscale, wconv, bconv, lng, lnb, *, start_pos):
    m = z.shape[0]
    bt = 16
    row = lambda a: a.reshape(1, -1)
    full2 = lambda r, c: pl.BlockSpec((r, c), lambda b: (0, 0))
    kern = functools.partial(_mixer_step_kernel, start_pos=start_pos)
    return pl.pallas_call(
        kern,
        grid=(m // bt,),
        in_specs=[
            pl.BlockSpec((bt, z.shape[1]), lambda b: (b, 0)),
            pl.BlockSpec((bt, POOL_BUF, POOL_W), lambda b: (b, 0, 0)),
            pl.BlockSpec((bt, CONV_BUF, CONV_W), lambda b: (b, 0, 0)),
            pl.BlockSpec((len(POOL_WINDOWS), POOL_GROUP, POOL_GROUP), lambda b: (0, 0, 0)),
            full2(1, POOL_W),
            full2(CONV_WIDTH, CONV_W),
            full2(1, CONV_W),
            full2(1, CONV_W),
            full2(1, CONV_W),
        ],
        out_specs=[
            pl.BlockSpec((bt, D_MODEL), lambda b: (b, 0)),
            pl.BlockSpec((bt, POOL_W), lambda b: (b, 0)),
            pl.BlockSpec((bt, CONV_W), lambda b: (b, 0)),
        ],
        out_shape=[
            jax.ShapeDtypeStruct((m, D_MODEL), BF16),
            jax.ShapeDtypeStruct((m, POOL_W), F32),
            jax.ShapeDtypeStruct((m, CONV_W), F32),
        ],
        compiler_params=_params(40, 1),
        name="mixer_step",
    )(z, pbuf, cbuf, wpool_bf, row(pscale), wconv, row(bconv), row(lng), row(lnb))


def _out_proj_kernel(m_ref, w_ref, x_ref, ga_ref, o_ref):
    o_ref[...] = x_ref[...] + ga_ref[...] * jnp.dot(m_ref[...], w_ref[...],
                                                    preferred_element_type=F32)


def _out_proj(mix, w_bf, x, ga, *, tm, rows_per_seq, per_row):
    m, d = mix.shape
    n = w_bf.shape[1]
    tn = 1024
    return pl.pallas_call(
        _out_proj_kernel,
        grid=(m // tm, n // tn),
        in_specs=[
            pl.BlockSpec((tm, d), lambda i, j: (i, 0)),
            pl.BlockSpec((d, tn), lambda i, j: (0, j)),
            pl.BlockSpec((tm, tn), lambda i, j: (i, j)),
            _mod_spec(per_row, tm, tn, rows_per_seq, lambda j: j),
        ],
        out_specs=pl.BlockSpec((tm, tn), lambda i, j: (i, j)),
        out_shape=jax.ShapeDtypeStruct((m, n), F32),
        compiler_params=_params(52, 2),
        name="out_proj",
    )(mix, w_bf, x, ga)


def _route_kernel(x_ref, g_ref, sc_ref, sh_ref, wr_ref, br_ref, cin_ref, h2_any, h2_ref, meta_ref,
                  gate_ref, cout_ref, run_scr, *, tm):
    del h2_any
    step = pl.program_id(0)

    @pl.when(step == 0)
    def _():
        run_scr[...] = cin_ref[...]

    h2 = _rmsnorm(x_ref[...], g_ref[...]) * (1.0 + sc_ref[...]) + sh_ref[...]
    for c in range(ROW_CHUNKS):
        h2_ref[pl.ds(c, tm, stride=ROW_PITCH), :] = h2[:, c * 128:(c + 1) * 128]
    for c in range(ROW_CHUNKS, ROW_PITCH):
        h2_ref[pl.ds(c, tm, stride=ROW_PITCH), :] = jnp.zeros((tm, 128), F32)
    logits = jnp.dot(h2.astype(BF16), wr_ref[...], preferred_element_type=F32) + br_ref[...]

    e_iota = lax.broadcasted_iota(I32, (tm, N_EXPERTS), 1).astype(F32)
    work = logits
    vals, idxs, sels = [], [], []
    for _ in range(TOP_K):
        mx = jnp.max(work, axis=1, keepdims=True)
        idx = jnp.min(jnp.where(work == mx, e_iota, float(N_EXPERTS)), axis=1, keepdims=True)
        sel = e_iota == idx
        vals.append(mx)
        idxs.append(idx)
        sels.append(sel)
        work = jnp.where(sel, -jnp.inf, work)

    exps = [jnp.exp(v - vals[0]) for v in vals]
    den = exps[0] + exps[1] + exps[2] + exps[3]

    onehot = jnp.zeros((tm, N_EXPERTS), F32)
    for sel in sels:
        onehot = onehot + jnp.where(sel, 1.0, 0.0)
    r_iota = lax.broadcasted_iota(I32, (tm, tm), 0)
    c_iota = lax.broadcasted_iota(I32, (tm, tm), 1)
    lower = jnp.where(c_iota < r_iota, 1.0, 0.0).astype(BF16)
    before = jnp.dot(lower, onehot.astype(BF16), preferred_element_type=F32) + run_scr[...]

    lane = lax.broadcasted_iota(I32, (tm, 128), 1)
    meta = jnp.zeros((tm, 128), F32)
    gate = jnp.zeros((tm, 128), F32)
    for k in range(TOP_K):
        rank = jnp.sum(jnp.where(sels[k], before, 0.0), axis=1, keepdims=True)
        meta = jnp.where(lane == k, idxs[k], meta)
        meta = jnp.where(lane == TOP_K + k, rank, meta)
        gate = jnp.where(lane == k, exps[k] / den, gate)
    meta_ref[...] = meta.astype(I32)
    gate_ref[...] = gate

    run_scr[...] = run_scr[...] + jnp.sum(onehot, axis=0, keepdims=True)
    cout_ref[...] = run_scr[...]


def _route(x1, g, sc, sh, wr_bf, br, counts_in, h2_all, *, tm, rows_per_seq, per_row, row_block0):
    m, d = x1.shape
    zero = lambda j: 0
    mod = lambda: (pl.BlockSpec((tm, d), lambda i: (i, 0)) if per_row else
                   pl.BlockSpec((None, 1, d), lambda i: ((i * tm) // rows_per_seq, 0, 0)))
    kern = functools.partial(_route_kernel, tm=tm)
    return pl.pallas_call(
        kern,
        grid=(m // tm,),
        in_specs=[
            pl.BlockSpec((tm, d), lambda i: (i, 0)),
            pl.BlockSpec((1, d), lambda i: (0, 0)),
            mod(),
            mod(),
            pl.BlockSpec((d, N_EXPERTS), lambda i: (0, 0)),
            pl.BlockSpec((1, N_EXPERTS), lambda i: (0, 0)),
            pl.BlockSpec((1, N_EXPERTS), lambda i: (0, 0)),
            pl.BlockSpec(memory_space=pl.ANY),
        ],
        out_specs=[
            pl.BlockSpec((tm * ROW_PITCH, 128), lambda i: (row_block0 + i, 0)),
            pl.BlockSpec((tm, 128), lambda i: (i, 0)),
            pl.BlockSpec((tm, 128), lambda i: (i, 0)),
            pl.BlockSpec((1, N_EXPERTS), lambda i: (0, 0)),
        ],
        out_shape=[
            jax.ShapeDtypeStruct(h2_all.shape, F32),
            jax.ShapeDtypeStruct((m, 128), I32),
            jax.ShapeDtypeStruct((m, 128), F32),
            jax.ShapeDtypeStruct((1, N_EXPERTS), F32),
        ],
        scratch_shapes=[pltpu.VMEM((1, N_EXPERTS), F32)],
        input_output_aliases={7: 0},
        compiler_params=_params(52, 1),
        name="route",
    )(x1, g, sc, sh, wr_bf, br, counts_in, h2_all)


def _gather_kernel(valid_ref, omap_ref, nxt_ref, slot_ref, tok0_ref, tokn_ref, h2_hbm, o_ref, buf,
                   sem):
    del omap_ref
    s = pl.program_id(0)

    def issue(tok_ref, slot):
        def body(r8, carry):
            for u in range(8):
                r = r8 * 8 + u
                src = h2_hbm.at[pl.ds(tok_ref[0, r], ROW_CHUNKS), :]
                dst = buf.at[slot, pl.ds(r * ROW_PITCH, ROW_CHUNKS), :]
                pltpu.make_async_copy(src, dst, sem.at[slot]).start(priority=u % 2)
            return carry

        lax.fori_loop(0, GATHER_ROWS // 8, body, 0)

    @pl.when(s == 0)
    def _():
        issue(tok0_ref, 0)

    @pl.when(valid_ref[s] > 0)
    def _():
        slot = slot_ref[s]

        @pl.when(nxt_ref[s] >= 0)
        def _():
            issue(tokn_ref, 1 - slot)

        nrow = GATHER_ROWS * ROW_CHUNKS
        pltpu.make_async_copy(h2_hbm.at[pl.ds(0, nrow), :], buf.at[slot, pl.ds(0, nrow), :],
                              sem.at[slot]).wait()
        for c in range(ROW_CHUNKS):
            piece = buf[slot, pl.ds(c, GATHER_ROWS, stride=ROW_PITCH), :]
            o_ref[:, c * 128:(c + 1) * 128] = piece.astype(BF16)


def _gather(valid, slot_row, h2_all, n_steps):
    d = D_MODEL
    steps = jnp.arange(n_steps, dtype=I32)
    live = valid > 0
    omap = lax.cummax(jnp.where(live, steps, 0))
    later = jnp.where(live, steps, n_steps)
    nxt = lax.cummin(jnp.concatenate([later[1:], jnp.full((1,), n_steps, I32)]), reverse=True)
    nxt = jnp.where(nxt >= n_steps, -1, nxt).astype(I32)
    slot = ((jnp.cumsum(live.astype(I32)) - 1) % 2).astype(I32)
    toks = slot_row.reshape(n_steps, 1, GATHER_ROWS)
    return pl.pallas_call(
        _gather_kernel,
        grid_spec=pltpu.PrefetchScalarGridSpec(
            num_scalar_prefetch=4,
            grid=(n_steps,),
            in_specs=[
                pl.BlockSpec((None, 1, GATHER_ROWS), lambda s, v, om, nx, sl: (0, 0, 0),
                             memory_space=pltpu.SMEM),
                pl.BlockSpec((None, 1, GATHER_ROWS),
                             lambda s, v, om, nx, sl: (jnp.maximum(nx[s], 0), 0, 0),
                             memory_space=pltpu.SMEM),
                pl.BlockSpec(memory_space=pl.ANY),
            ],
            out_specs=pl.BlockSpec((GATHER_ROWS, d), lambda s, v, om, nx, sl: (om[s], 0)),
            scratch_shapes=[pltpu.VMEM((2, GATHER_ROWS * ROW_PITCH, 128), F32),
                            pltpu.SemaphoreType.DMA((2,))],
        ),
        out_shape=jax.ShapeDtypeStruct((n_steps * GATHER_ROWS, d), BF16),
        compiler_params=_params(32, 1),
        name="gather",
    )(valid, omap, nxt, slot, toks, toks, h2_all)


MOE_ONE_DOT_FROM = 16


def _for_row_pieces(nsub, piece, prologue):
    for n in range(MOE_ONE_DOT_FROM, MOE_NSUB + 1):
        @pl.when(nsub == n)
        def _(n=n):
            prologue()
            piece(0, n * MOE_SUB)

    small = nsub < MOE_ONE_DOT_FROM

    @pl.when(small)
    def _():
        prologue()

    units = MOE_ONE_DOT_FROM // 2
    while units >= 1:
        @pl.when(jnp.logical_and(small, (nsub & units) != 0))
        def _(units=units):
            start = (nsub & (-2 * units)) * MOE_SUB
            piece(pl.multiple_of(start, units * MOE_SUB), units * MOE_SUB)

        units //= 2


def _moe_up_kernel(be_ref, nsub_ref, bmap_ref, xs_ref, wg_ref, wu_ref, bg_ref, bu_ref, act_ref,
                   w_scr):
    del be_ref, bmap_ref
    nsub = nsub_ref[pl.program_id(0)]
    tf = MOE_TF

    @pl.when(nsub > 0)
    def _():
        def cast_weights():
            w_scr[:, 0:tf] = wg_ref[...].astype(BF16)
            w_scr[:, tf:2 * tf] = wu_ref[...].astype(BF16)

        def piece(row0, size):
            rows = pl.ds(row0, size)
            gu = jnp.dot(xs_ref[rows, :], w_scr[...], preferred_element_type=F32)
            g = jnp.minimum(gu[:, 0:tf] + bg_ref[...], SWIGLU_LIMIT)
            u = jnp.clip(gu[:, tf:2 * tf] + bu_ref[...], -SWIGLU_LIMIT, SWIGLU_LIMIT)
            act = (u + 1.0) * (g * jax.nn.sigmoid(SWIGLU_ALPHA * g))
            act_ref[rows, :] = act.astype(BF16)

        _for_row_pieces(nsub, piece, cast_weights)

        def fill(j, carry):
            rows = pl.ds(pl.multiple_of(j * MOE_SUB, MOE_SUB), MOE_SUB)
            act_ref[rows, :] = jnp.zeros((MOE_SUB, tf), BF16)
            return carry

        lax.fori_loop(nsub, MOE_NSUB, fill, 0)


def _moe_up(be, nsub, bmap, xs, w_gu, b_gu, n_blocks, n_live):
    d = xs.shape[1]
    tf = MOE_TF
    nf = D_FF // tf
    live = lambda b, ns: ns[b] > 0
    return pl.pallas_call(
        _moe_up_kernel,
        grid_spec=pltpu.PrefetchScalarGridSpec(
            num_scalar_prefetch=3,
            grid=(n_live, nf),
            in_specs=[
                pl.BlockSpec((MOE_TM, d), lambda b, f, be, ns, bm: (bm[b], 0)),
                pl.BlockSpec((None, d, tf),
                             lambda b, f, be, ns, bm: (be[b], 0, jnp.where(live(b, ns), f, nf - 1))),
                pl.BlockSpec((None, d, tf),
                             lambda b, f, be, ns, bm: (be[b], 0,
                                                       nf + jnp.where(live(b, ns), f, nf - 1))),
                pl.BlockSpec((None, 1, tf),
                             lambda b, f, be, ns, bm: (be[b], 0, jnp.where(live(b, ns), f, nf - 1))),
                pl.BlockSpec((None, 1, tf),
                             lambda b, f, be, ns, bm: (be[b], 0,
                                                       nf + jnp.where(live(b, ns), f, nf - 1))),
            ],
            out_specs=pl.BlockSpec(
                (MOE_TM, tf),
                lambda b, f, be, ns, bm: (jnp.where(live(b, ns), b, n_blocks),
                                          jnp.where(live(b, ns), f, 0))),
            scratch_shapes=[pltpu.VMEM((d, 2 * tf), BF16)],
        ),
        out_shape=jax.ShapeDtypeStruct(((n_blocks + 1) * MOE_TM, D_FF), BF16),
        compiler_params=_params(56, 2),
        name="moe_up",
    )(be, nsub, bmap, xs, w_gu, w_gu, b_gu, b_gu)


def _moe_down_kernel(be_ref, nsub_ref, bmap_ref, act_ref, wd_ref, bd_ref, ys_ref, w_scr):
    del be_ref, bmap_ref
    nsub = nsub_ref[pl.program_id(0)]

    @pl.when(nsub > 0)
    def _():
        def cast_weights():
            w_scr[...] = wd_ref[...].astype(BF16)

        def piece(row0, size):
            rows = pl.ds(row0, size)
            y = jnp.dot(act_ref[rows, :], w_scr[...], preferred_element_type=F32)
            ys_ref[rows, :] = y + bd_ref[...]

        _for_row_pieces(nsub, piece, cast_weights)

        def fill(j, carry):
            rows = pl.ds(pl.multiple_of(j * MOE_SUB, MOE_SUB), MOE_SUB)
            ys_ref[rows, :] = jnp.zeros((MOE_SUB, MOE_TN), F32)
            return carry

        lax.fori_loop(nsub, MOE_NSUB, fill, 0)


def _moe_down(be, nsub, bmap, act, w_down, b_down, n_blocks, n_live):
    dff = act.shape[1]
    tn = MOE_TN
    nn = D_MODEL // tn
    live = lambda b, ns: ns[b] > 0
    return pl.pallas_call(
        _moe_down_kernel,
        grid_spec=pltpu.PrefetchScalarGridSpec(
            num_scalar_prefetch=3,
            grid=(n_live, nn),
            in_specs=[
                pl.BlockSpec((MOE_TM, dff), lambda b, n, be, ns, bm: (bm[b], 0)),
                pl.BlockSpec((None, dff, tn),
                             lambda b, n, be, ns, bm: (be[b], 0, jnp.where(live(b, ns), n, nn - 1))),
                pl.BlockSpec((None, 1, tn),
                             lambda b, n, be, ns, bm: (be[b], 0, jnp.where(live(b, ns), n, nn - 1))),
            ],
            out_specs=pl.BlockSpec(
                (MOE_TM, tn),
                lambda b, n, be, ns, bm: (jnp.where(live(b, ns), b, n_blocks),
                                          jnp.where(live(b, ns), n, 0))),
            scratch_shapes=[pltpu.VMEM((dff, tn), BF16)],
        ),
        out_shape=jax.ShapeDtypeStruct(((n_blocks + 1) * MOE_TM, D_MODEL), F32),
        compiler_params=_params(56, 2),
        name="moe_down",
    )(be, nsub, bmap, act, w_down, b_down)


def _combine_kernel(dest0_ref, destn_ref, x_ref, gate_ref, ga_ref, gf_ref, ys_hbm, o_ref, buf, sem):
    i = pl.program_id(0)
    slot = i % 2

    def issue(dest_ref, slot):
        def body(r8, carry):
            for u in range(8):
                r = r8 * 8 + u
                src = ys_hbm.at[pl.ds(dest_ref[0, r], 1), :]
                dst = buf.at[slot, pl.ds(r, 1), :]
                pltpu.make_async_copy(src, dst, sem.at[slot]).start(priority=u % 2)
            return carry

        lax.fori_loop(0, GATHER_ROWS // 8, body, 0)

    @pl.when(i == 0)
    def _():
        issue(dest0_ref, 0)

    @pl.when(i + 1 < pl.num_programs(0))
    def _():
        issue(destn_ref, 1 - slot)

    pltpu.make_async_copy(ys_hbm.at[pl.ds(0, GATHER_ROWS), :], buf.at[slot], sem.at[slot]).wait()

    acc = gate_ref[:, 0:1] * buf[slot, 0:COMBINE_TOK, :]
    for k in range(1, TOP_K):
        acc = acc + gate_ref[:, k:k + 1] * buf[slot, k * COMBINE_TOK:(k + 1) * COMBINE_TOK, :]
    x2 = x_ref[...] + ga_ref[...] * acc
    o_ref[...] = _rmsnorm(x2, gf_ref[...])


def _combine(dest_tiles, x1, gates, ga, g_final, ys, *, rows_per_seq, per_row):
    m, d = x1.shape
    tm = COMBINE_TOK
    nt = m // tm
    ga_spec = (pl.BlockSpec((tm, d), lambda i: (i, 0)) if per_row else
               pl.BlockSpec((None, 1, d), lambda i: ((i * tm) // rows_per_seq, 0, 0)))
    return pl.pallas_call(
        _combine_kernel,
        grid=(nt,),
        in_specs=[
            pl.BlockSpec((None, 1, GATHER_ROWS), lambda i: (0, 0, 0), memory_space=pltpu.SMEM),
            pl.BlockSpec((None, 1, GATHER_ROWS), lambda i: (jnp.minimum(i + 1, nt - 1), 0, 0),
                         memory_space=pltpu.SMEM),
            pl.BlockSpec((tm, d), lambda i: (i, 0)),
            pl.BlockSpec((tm, 128), lambda i: (i, 0)),
            ga_spec,
            pl.BlockSpec((1, d), lambda i: (0, 0)),
            pl.BlockSpec(memory_space=pl.ANY),
        ],
        out_specs=pl.BlockSpec((tm, d), lambda i: (i, 0)),
        out_shape=jax.ShapeDtypeStruct((m, d), F32),
        scratch_shapes=[pltpu.VMEM((2, GATHER_ROWS, d), F32), pltpu.SemaphoreType.DMA((2,))],
        compiler_params=_params(32, 1),
        name="combine",
    )(dest_tiles, dest_tiles, x1, gates, ga, g_final.reshape(1, d), ys)


def _routing_tables(counts, idx, rank, n_blocks):
    nb = (counts + MOE_TM - 1) // MOE_TM
    cum_nb = jnp.cumsum(nb)
    blk_start = cum_nb - nb
    total = cum_nb[-1]
    dest = (blk_start * MOE_TM)[idx] + rank

    b_ids = jnp.arange(n_blocks, dtype=I32)
    bmap = jnp.maximum(jnp.minimum(b_ids, total - 1), 0)
    be = jnp.minimum(jnp.sum(bmap[:, None] >= cum_nb[None, :], axis=1), N_EXPERTS - 1).astype(I32)
    rows = jnp.clip(counts[be] - (b_ids - blk_start[be]) * MOE_TM, 0, MOE_TM)
    rows = jnp.where(b_ids < total, rows, 0)
    nsub = ((rows + MOE_SUB - 1) // MOE_SUB).astype(I32)
    return dest, be, nsub, bmap.astype(I32), total.astype(I32)


def kernel(x_prompt, x_sample, state_pool, state_conv, c_prompt, c_sample, w_ada, b_ada, g_norm1,
           w_in, w_pool, pool_scale, w_conv, b_conv, ln_g, ln_b, w_out, g_norm2, w_router,
           b_router, w_gu, b_gu, w_down, b_down, g_final):
    assert w_ada.shape[0] == 1, "single-layer trunk"
    nbp, seq, d = x_prompt.shape
    nbs, dec_seq, _ = x_sample.shape
    assert dec_seq == 1
    past_len = 16384
    tp, ts = nbp * seq, nbs
    t_all = tp + ts

    n_c = nbp + nbs
    pad_c = (-n_c) % 16
    c_all = jnp.concatenate([c_prompt, c_sample, jnp.zeros((pad_c, d), F32)], axis=0)
    mod = _ada(c_all, w_ada[0], b_ada[0])
    mod_p = [mod[:nbp, k * d:(k + 1) * d].reshape(nbp, 1, d) for k in range(N_MOD)]
    mod_s = [mod[nbp:n_c, k * d:(k + 1) * d] for k in range(N_MOD)]

    w_in_bf = w_in[0].astype(BF16)
    w_out_bf = w_out[0].astype(BF16)
    w_pool_bf = w_pool[0].astype(BF16)
    wr_bf = w_router[0].astype(BF16)
    row = lambda a: a.reshape(1, -1)

    xp = x_prompt.reshape(tp, d)
    xs_tok = x_sample.reshape(ts, d)
    tm = 512

    z_p = _in_proj(xp, row(g_norm1[0]), mod_p[1], mod_p[0], w_in_bf,
                   tm=tm, rows_per_seq=seq, per_row=False)
    z_s = _in_proj(xs_tok, row(g_norm1[0]), mod_s[1], mod_s[0], w_in_bf,
                   tm=ts, rows_per_seq=1, per_row=True)
    mix_p, npool_p, nconv_p = _mixer_seq(
        z_p, jnp.zeros((nbp, POOL_BUF, POOL_W), F32), jnp.zeros((nbp, CONV_BUF, CONV_W), F32),
        w_pool_bf, pool_scale[0], w_conv[0], b_conv[0], ln_g[0], ln_b[0],
        seq_len=seq, start_pos=0)
    mix_s, u_s, a_s = _mixer_step(
        z_s, state_pool[0], state_conv[0], w_pool_bf, pool_scale[0], w_conv[0], b_conv[0],
        ln_g[0], ln_b[0], start_pos=past_len)
    x1_p = _out_proj(mix_p, w_out_bf, xp, mod_p[2], tm=tm, rows_per_seq=seq, per_row=False)
    x1_s = _out_proj(mix_s, w_out_bf, xs_tok, mod_s[2], tm=ts, rows_per_seq=1, per_row=True)

    h2_all = jnp.zeros((t_all * ROW_PITCH, 128), F32)
    counts0 = jnp.zeros((1, N_EXPERTS), F32)
    h2_all, meta_p, gate_p, counts1 = _route(
        x1_p, row(g_norm2[0]), mod_p[4], mod_p[3], wr_bf, row(b_router[0]), counts0, h2_all,
        tm=tm, rows_per_seq=seq, per_row=False, row_block0=0)
    h2_all, meta_s, gate_s, counts2 = _route(
        x1_s, row(g_norm2[0]), mod_s[4], mod_s[3], wr_bf, row(b_router[0]), counts1, h2_all,
        tm=ts, rows_per_seq=1, per_row=True, row_block0=tp // ts)

    n_assign = t_all * TOP_K
    n_blocks = n_assign // MOE_TM + N_EXPERTS
    counts = counts2[0].astype(I32)
    idx = jnp.concatenate([meta_p[:, 0:TOP_K], meta_s[:, 0:TOP_K]], axis=0)
    rank = jnp.concatenate([meta_p[:, TOP_K:2 * TOP_K], meta_s[:, TOP_K:2 * TOP_K]], axis=0)
    dest, be, nsub, bmap, n_live = _routing_tables(counts, idx, rank, n_blocks)

    n_gsteps = n_blocks * (MOE_TM // GATHER_ROWS)
    tok_rows = jnp.broadcast_to((jnp.arange(t_all, dtype=I32) * ROW_PITCH)[:, None],
                                (t_all, TOP_K))
    slot_row = jnp.zeros((n_gsteps * GATHER_ROWS,), I32).at[dest.reshape(-1)].set(
        tok_rows.reshape(-1))
    gstep_row0 = jnp.arange(MOE_TM // GATHER_ROWS, dtype=I32) * GATHER_ROWS
    gvalid = (gstep_row0[None, :] < nsub[:, None] * MOE_SUB).astype(I32).reshape(-1)

    xs = _gather(gvalid, slot_row, h2_all, n_gsteps)
    act = _moe_up(be, nsub, bmap, xs, w_gu[0], b_gu[0].reshape(N_EXPERTS, 1, 2 * D_FF), n_blocks,
                  n_live)
    ys = _moe_down(be, nsub, bmap, act, w_down[0], b_down[0].reshape(N_EXPERTS, 1, d), n_blocks,
                   n_live)

    def dest_tiles(dst):
        n = dst.shape[0] // COMBINE_TOK
        return dst.reshape(n, COMBINE_TOK, TOP_K).transpose(0, 2, 1).reshape(n, 1, GATHER_ROWS)

    y_p = _combine(dest_tiles(dest[:tp]), x1_p, gate_p, mod_p[5], g_final, ys,
                   rows_per_seq=seq, per_row=False)
    y_s = _combine(dest_tiles(dest[tp:]), x1_s, gate_s, mod_s[5], g_final, ys,
                   rows_per_seq=1, per_row=True)

    new_pool_s = jnp.concatenate([state_pool[0][:, 1:], u_s[:, None, :]], axis=1)
    new_conv_s = jnp.concatenate([state_conv[0][:, 1:], a_s[:, None, :]], axis=1)
    return (y_p.reshape(nbp, seq, d), y_s.reshape(nbs, 1, d), npool_p[None], nconv_p[None],
            new_pool_s[None], new_conv_s[None])
```

```python
import functools

import jax
import jax.numpy as jnp
from jax import lax
from jax.experimental import pallas as pl
from jax.experimental.pallas import tpu as pltpu

F32 = jnp.float32
BF16 = jnp.bfloat16
I32 = jnp.int32

D_MODEL = 4096
POOL_W = 2048
CONV_W = 2048
POOL_WINDOWS = (2, 4, 8, 16)
POOL_GROUP = 512
POOL_BUF = 15
CONV_WIDTH = 31
CONV_BUF = 30
CONV_HEADS = 4
CONV_HEAD_DIM = 512
N_EXPERTS = 32
TOP_K = 4
D_FF = 4096
SWIGLU_LIMIT = 7.0
SWIGLU_ALPHA = 1.702
N_MOD = 6
EPS = 1e-5

MIB = 1024 * 1024

MOE_TM = 1280
MOE_SUB = 128
MOE_NSUB = MOE_TM // MOE_SUB
MOE_TF = 256
MOE_TN = 512
GATHER_ROWS = 256
COMBINE_TOK = GATHER_ROWS // TOP_K
ROW_CHUNKS = D_MODEL // 128
ROW_PITCH = ROW_CHUNKS + 4


def _params(vmem_mib, n_axes):
    return pltpu.CompilerParams(
        dimension_semantics=("arbitrary",) * n_axes,
        vmem_limit_bytes=vmem_mib * MIB,
    )


def _ada_kernel(c_ref, w_ref, b_ref, o_ref):
    c = c_ref[...]
    s = (c * jax.nn.sigmoid(c)).astype(BF16)
    o_ref[...] = jnp.dot(s, w_ref[...].astype(BF16), preferred_element_type=F32) + b_ref[...]


def _ada(c_all, w_ada, b_ada):
    m, d = c_all.shape
    n = w_ada.shape[1]
    tn = 512
    return pl.pallas_call(
        _ada_kernel,
        grid=(n // tn,),
        in_specs=[
            pl.BlockSpec((m, d), lambda j: (0, 0)),
            pl.BlockSpec((d, tn), lambda j: (0, j)),
            pl.BlockSpec((1, tn), lambda j: (0, j)),
        ],
        out_specs=pl.BlockSpec((m, tn), lambda j: (0, j)),
        out_shape=jax.ShapeDtypeStruct((m, n), F32),
        compiler_params=_params(40, 1),
        name="ada",
    )(c_all, w_ada, b_ada.reshape(1, n))


def _rmsnorm(x, g):
    return (x * lax.rsqrt(jnp.mean(x * x, axis=-1, keepdims=True) + EPS)) * g


def _in_proj_kernel(x_ref, g_ref, sc_ref, sh_ref, w_ref, z_ref, h_scr):
    def matmul():
        z_ref[...] = jnp.dot(h_scr[...], w_ref[...], preferred_element_type=F32)

    @pl.when(pl.program_id(1) == 0)
    def _():
        h = _rmsnorm(x_ref[...], g_ref[...]) * (1.0 + sc_ref[...]) + sh_ref[...]
        h_scr[...] = h.astype(BF16)
        matmul()

    @pl.when(pl.program_id(1) != 0)
    def _():
        matmul()


def _mod_spec(per_row, tm, tn, rows_per_seq, col_of):
    if per_row:
        return pl.BlockSpec((tm, tn), lambda i, j: (i, col_of(j)))
    return pl.BlockSpec((None, 1, tn), lambda i, j: ((i * tm) // rows_per_seq, 0, col_of(j)))


def _in_proj(x, g, sc, sh, w_bf, *, tm, rows_per_seq, per_row):
    m, d = x.shape
    n = w_bf.shape[1]
    tn = 1024
    zero = lambda j: 0
    return pl.pallas_call(
        _in_proj_kernel,
        grid=(m // tm, n // tn),
        in_specs=[
            pl.BlockSpec((tm, d), lambda i, j: (i, 0)),
            pl.BlockSpec((1, d), lambda i, j: (0, 0)),
            _mod_spec(per_row, tm, d, rows_per_seq, zero),
            _mod_spec(per_row, tm, d, rows_per_seq, zero),
            pl.BlockSpec((d, tn), lambda i, j: (0, j)),
        ],
        out_specs=pl.BlockSpec((tm, tn), lambda i, j: (i, j)),
        out_shape=jax.ShapeDtypeStruct((m, n), F32),
        scratch_shapes=[pltpu.VMEM((tm, d), BF16)],
        compiler_params=_params(58, 2),
        name="in_proj",
    )(x, g, sc, sh, w_bf)


def _pool_counts(pos, w):
    return jnp.minimum(jnp.float32(w), pos + 1.0)


def _layernorm_silu(y, g, b):
    mu = jnp.mean(y, axis=-1, keepdims=True)
    dlt = y - mu
    var = jnp.mean(dlt * dlt, axis=-1, keepdims=True)
    yn = dlt * lax.rsqrt(var + EPS) * g + b
    return yn * jax.nn.sigmoid(yn)


def _mixer_seq_kernel(z_ref, pbuf_ref, cbuf_ref, wpool_ref, pscale_ref, wconv_ref, bconv_ref,
                      lng_ref, lnb_ref, mix_ref, npool_ref, nconv_ref, extu, exta, shf, *,
                      tl, start_pos):
    i = pl.program_id(1)
    hu, ha = POOL_BUF + 1, CONV_BUF + 2

    @pl.when(i == 0)
    def _():
        extu[0:1, :] = jnp.zeros((1, POOL_W), F32)
        exta[0:2, :] = jnp.zeros((2, CONV_W), F32)
        extu[1:hu, :] = pbuf_ref[...]
        exta[2:ha, :] = cbuf_ref[...]

    @pl.when(i > 0)
    def _():
        extu[0:hu, :] = extu[tl:tl + hu, :]
        exta[0:ha, :] = exta[tl:tl + ha, :]

    extu[hu:hu + tl, :] = z_ref[:, 0:POOL_W]
    exta[ha:ha + tl, :] = (z_ref[:, POOL_W:POOL_W + CONV_W]
                           * jax.nn.sigmoid(z_ref[:, POOL_W + CONV_W:POOL_W + 2 * CONV_W]))

    pos = (lax.broadcasted_iota(I32, (tl, 1), 0) + (i * tl + start_pos)).astype(F32)
    for gi, w in enumerate(POOL_WINDOWS):
        sl = slice(gi * POOL_GROUP, (gi + 1) * POOL_GROUP)
        u = extu[hu:hu + tl, sl]
        s = u
        for j in range(1, w):
            s = s + extu[hu - j:hu - j + tl, sl]
        dlt = s * (1.0 / _pool_counts(pos, w)) - u
        y = jnp.dot(dlt.astype(BF16), wpool_ref[gi], preferred_element_type=F32)
        mix_ref[:, sl] = (y * pscale_ref[:, sl]).astype(BF16)

    ncopy = tl + 3 * 8
    for hi in range(CONV_HEADS):
        sl = slice(hi * CONV_HEAD_DIM, (hi + 1) * CONV_HEAD_DIM)
        for b in range(1, 8):
            shf[b - 1, 0:ncopy, :] = exta[b:b + ncopy, sl]
        acc = None
        for j in range(CONV_WIDTH):
            a, b = divmod(2 + j, 8)
            if b == 0:
                src = exta[8 * a:8 * a + tl, sl]
            else:
                src = shf[b - 1, 8 * a:8 * a + tl, :]
            term = src * wconv_ref[j:j + 1, sl]
            acc = term if acc is None else acc + term
        y = _layernorm_silu(acc + bconv_ref[:, sl], lng_ref[:, sl], lnb_ref[:, sl])
        mix_ref[:, POOL_W + hi * CONV_HEAD_DIM:POOL_W + (hi + 1) * CONV_HEAD_DIM] = y.astype(BF16)

    @pl.when(i == pl.num_programs(1) - 1)
    def _():
        npool_ref[...] = extu[tl + 1:tl + hu, :]
        nconv_ref[...] = exta[tl + 2:tl + ha, :]


def _mixer_seq(z, pbuf, cbuf, wpool_bf, pscale, wconv, bconv, lng, lnb, *, seq_len, start_pos):
    m = z.shape[0]
    nb = m // seq_len
    tl = 256
    nt = seq_len // tl
    row = lambda a: a.reshape(1, -1)
    full2 = lambda r, c: pl.BlockSpec((r, c), lambda b, i: (0, 0))
    kern = functools.partial(_mixer_seq_kernel, tl=tl, start_pos=start_pos)
    return pl.pallas_call(
        kern,
        grid=(nb, nt),
        in_specs=[
            pl.BlockSpec((tl, z.shape[1]), lambda b, i: (b * nt + i, 0)),
            pl.BlockSpec((None, POOL_BUF, POOL_W), lambda b, i: (b, 0, 0)),
            pl.BlockSpec((None, CONV_BUF, CONV_W), lambda b, i: (b, 0, 0)),
            pl.BlockSpec((len(POOL_WINDOWS), POOL_GROUP, POOL_GROUP), lambda b, i: (0, 0, 0)),
            full2(1, POOL_W),
            full2(CONV_WIDTH, CONV_W),
            full2(1, CONV_W),
            full2(1, CONV_W),
            full2(1, CONV_W),
        ],
        out_specs=[
            pl.BlockSpec((tl, D_MODEL), lambda b, i: (b * nt + i, 0)),
            pl.BlockSpec((None, POOL_BUF, POOL_W), lambda b, i: (b, 0, 0)),
            pl.BlockSpec((None, CONV_BUF, CONV_W), lambda b, i: (b, 0, 0)),
        ],
        out_shape=[
            jax.ShapeDtypeStruct((m, D_MODEL), BF16),
            jax.ShapeDtypeStruct((nb, POOL_BUF, POOL_W), F32),
            jax.ShapeDtypeStruct((nb, CONV_BUF, CONV_W), F32),
        ],
        scratch_shapes=[
            pltpu.VMEM((POOL_BUF + 1 + tl, POOL_W), F32),
            pltpu.VMEM((CONV_BUF + 2 + tl, CONV_W), F32),
            pltpu.VMEM((7, tl + 3 * 8, CONV_HEAD_DIM), F32),
        ],
        compiler_params=_params(40, 2),
        name="mixer_seq",
    )(z, pbuf, cbuf, wpool_bf, row(pscale), wconv, row(bconv), row(lng), row(lnb))


def _mixer_step_kernel(z_ref, pbuf_ref, cbuf_ref, wpool_ref, pscale_ref, wconv_ref, bconv_ref,
                       lng_ref, lnb_ref, mix_ref, u_ref, a_ref, *, start_pos):
    u_all = z_ref[:, 0:POOL_W]
    a_all = (z_ref[:, POOL_W:POOL_W + CONV_W]
             * jax.nn.sigmoid(z_ref[:, POOL_W + CONV_W:POOL_W + 2 * CONV_W]))
    u_ref[...] = u_all
    a_ref[...] = a_all

    for gi, w in enumerate(POOL_WINDOWS):
        sl = slice(gi * POOL_GROUP, (gi + 1) * POOL_GROUP)
        u = u_all[:, sl]
        s = u
        for j in range(1, w):
            s = s + pbuf_ref[:, POOL_BUF - j, sl]
        cnt = min(float(w), float(start_pos) + 1.0)
        dlt = s / cnt - u
        y = jnp.dot(dlt.astype(BF16), wpool_ref[gi], preferred_element_type=F32)
        mix_ref[:, sl] = (y * pscale_ref[:, sl]).astype(BF16)

    for hi in range(CONV_HEADS):
        sl = slice(hi * CONV_HEAD_DIM, (hi + 1) * CONV_HEAD_DIM)
        acc = a_all[:, sl] * wconv_ref[CONV_BUF:CONV_BUF + 1, sl]
        for j in range(CONV_BUF):
            acc = acc + cbuf_ref[:, j, sl] * wconv_ref[j:j + 1, sl]
        y = _layernorm_silu(acc + bconv_ref[:, sl], lng_ref[:, sl], lnb_ref[:, sl])
        mix_ref[:, POOL_W + hi * CONV_HEAD_DIM:POOL_W + (hi + 1) * CONV_HEAD_DIM] = y.astype(BF16)


def _mixer_step(z, pbuf, cbuf, wpool_bf, pscale, wconv, bconv, lng, lnb, *, start_pos):
    m = z.shape[0]
    bt = 16
    row = lambda a: a.reshape(1, -1)
    full2 = lambda r, c: pl.BlockSpec((r, c), lambda b: (0, 0))
    kern = functools.partial(_mixer_step_kernel, start_pos=start_pos)
    return pl.pallas_call(
        kern,
        grid=(m // bt,),
        in_specs=[
            pl.BlockSpec((bt, z.shape[1]), lambda b: (b, 0)),
            pl.BlockSpec((bt, POOL_BUF, POOL_W), lambda b: (b, 0, 0)),
            pl.BlockSpec((bt, CONV_BUF, CONV_W), lambda b: (b, 0, 0)),
            pl.BlockSpec((len(POOL_WINDOWS), POOL_GROUP, POOL_GROUP), lambda b: (0, 0, 0)),
            full2(1, POOL_W),
            full2(CONV_WIDTH, CONV_W),
            full2(1, CONV_W),
            full2(1, CONV_W),
            full2(1, CONV_W),
        ],
        out_specs=[
            pl.BlockSpec((bt, D_MODEL), lambda b: (b, 0)),
            pl.BlockSpec((bt, POOL_W), lambda b: (b, 0)),
            pl.BlockSpec((bt, CONV_W), lambda b: (b, 0)),
        ],
        out_shape=[
            jax.ShapeDtypeStruct((m, D_MODEL), BF16),
            jax.ShapeDtypeStruct((m, POOL_W), F32),
            jax.ShapeDtypeStruct((m, CONV_W), F32),
        ],
        compiler_params=_params(40, 1),
        name="mixer_step",
    )(z, pbuf, cbuf, wpool_bf, row(pscale), wconv, row(bconv), row(lng), row(lnb))


def _out_proj_kernel(m_ref, w_ref, x_ref, ga_ref, o_ref):
    o_ref[...] = x_ref[...] + ga_ref[...] * jnp.dot(m_ref[...], w_ref[...],
                                                    preferred_element_type=F32)


def _out_proj(mix, w_bf, x, ga, *, tm, rows_per_seq, per_row):
    m, d = mix.shape
    n = w_bf.shape[1]
    tn = 1024
    return pl.pallas_call(
        _out_proj_kernel,
        grid=(m // tm, n // tn),
        in_specs=[
            pl.BlockSpec((tm, d), lambda i, j: (i, 0)),
            pl.BlockSpec((d, tn), lambda i, j: (0, j)),
            pl.BlockSpec((tm, tn), lambda i, j: (i, j)),
            _mod_spec(per_row, tm, tn, rows_per_seq, lambda j: j),
        ],
        out_specs=pl.BlockSpec((tm, tn), lambda i, j: (i, j)),
        out_shape=jax.ShapeDtypeStruct((m, n), F32),
        compiler_params=_params(52, 2),
        name="out_proj",
    )(mix, w_bf, x, ga)


def _route_kernel(x_ref, g_ref, sc_ref, sh_ref, wr_ref, br_ref, cin_ref, h2_any, h2_ref, meta_ref,
                  gate_ref, cout_ref, run_scr, *, tm):
    del h2_any
    step = pl.program_id(0)

    @pl.when(step == 0)
    def _():
        run_scr[...] = cin_ref[...]

    h2 = _rmsnorm(x_ref[...], g_ref[...]) * (1.0 + sc_ref[...]) + sh_ref[...]
    for c in range(ROW_CHUNKS):
        h2_ref[pl.ds(c, tm, stride=ROW_PITCH), :] = h2[:, c * 128:(c + 1) * 128]
    for c in range(ROW_CHUNKS, ROW_PITCH):
        h2_ref[pl.ds(c, tm, stride=ROW_PITCH), :] = jnp.zeros((tm, 128), F32)
    logits = jnp.dot(h2.astype(BF16), wr_ref[...], preferred_element_type=F32) + br_ref[...]

    e_iota = lax.broadcasted_iota(I32, (tm, N_EXPERTS), 1).astype(F32)
    work = logits
    vals, idxs, sels = [], [], []
    for _ in range(TOP_K):
        mx = jnp.max(work, axis=1, keepdims=True)
        idx = jnp.min(jnp.where(work == mx, e_iota, float(N_EXPERTS)), axis=1, keepdims=True)
        sel = e_iota == idx
        vals.append(mx)
        idxs.append(idx)
        sels.append(sel)
        work = jnp.where(sel, -jnp.inf, work)

    exps = [jnp.exp(v - vals[0]) for v in vals]
    den = exps[0] + exps[1] + exps[2] + exps[3]

    onehot = jnp.zeros((tm, N_EXPERTS), F32)
    for sel in sels:
        onehot = onehot + jnp.where(sel, 1.0, 0.0)
    r_iota = lax.broadcasted_iota(I32, (tm, tm), 0)
    c_iota = lax.broadcasted_iota(I32, (tm, tm), 1)
    lower = jnp.where(c_iota < r_iota, 1.0, 0.0).astype(BF16)
    before = jnp.dot(lower, onehot.astype(BF16), preferred_element_type=F32) + run_scr[...]

    lane = lax.broadcasted_iota(I32, (tm, 128), 1)
    meta = jnp.zeros((tm, 128), F32)
    gate = jnp.zeros((tm, 128), F32)
    for k in range(TOP_K):
        rank = jnp.sum(jnp.where(sels[k], before, 0.0), axis=1, keepdims=True)
        meta = jnp.where(lane == k, idxs[k], meta)
        meta = jnp.where(lane == TOP_K + k, rank, meta)
        gate = jnp.where(lane == k, exps[k] / den, gate)
    meta_ref[...] = meta.astype(I32)
    gate_ref[...] = gate

    run_scr[...] = run_scr[...] + jnp.sum(onehot, axis=0, keepdims=True)
    cout_ref[...] = run_scr[...]


def _route(x1, g, sc, sh, wr_bf, br, counts_in, h2_all, *, tm, rows_per_seq, per_row, row_block0):
    m, d = x1.shape
    zero = lambda j: 0
    mod = lambda: (pl.BlockSpec((tm, d), lambda i: (i, 0)) if per_row else
                   pl.BlockSpec((None, 1, d), lambda i: ((i * tm) // rows_per_seq, 0, 0)))
    kern = functools.partial(_route_kernel, tm=tm)
    return pl.pallas_call(
        kern,
        grid=(m // tm,),
        in_specs=[
            pl.BlockSpec((tm, d), lambda i: (i, 0)),
            pl.BlockSpec((1, d), lambda i: (0, 0)),
            mod(),
            mod(),
            pl.BlockSpec((d, N_EXPERTS), lambda i: (0, 0)),
            pl.BlockSpec((1, N_EXPERTS), lambda i: (0, 0)),
            pl.BlockSpec((1, N_EXPERTS), lambda i: (0, 0)),
            pl.BlockSpec(memory_space=pl.ANY),
        ],
        out_specs=[
            pl.BlockSpec((tm * ROW_PITCH, 128), lambda i: (row_block0 + i, 0)),
            pl.BlockSpec((tm, 128), lambda i: (i, 0)),
            pl.BlockSpec((tm, 128), lambda i: (i, 0)),
            pl.BlockSpec((1, N_EXPERTS), lambda i: (0, 0)),
        ],
        out_shape=[
            jax.ShapeDtypeStruct(h2_all.shape, F32),
            jax.ShapeDtypeStruct((m, 128), I32),
            jax.ShapeDtypeStruct((m, 128), F32),
            jax.ShapeDtypeStruct((1, N_EXPERTS), F32),
        ],
        scratch_shapes=[pltpu.VMEM((1, N_EXPERTS), F32)],
        input_output_aliases={7: 0},
        compiler_params=_params(52, 1),
        name="route",
    )(x1, g, sc, sh, wr_bf, br, counts_in, h2_all)


def _for_row_pieces(nsub, piece, prologue):
    top = 1
    while top * 2 <= MOE_NSUB:
        top *= 2
    has_top = (nsub & top) != 0

    @pl.when(has_top)
    def _():
        prologue()
        piece(0, top * MOE_SUB)

    @pl.when(jnp.logical_not(has_top))
    def _():
        prologue()

    units = top // 2
    while units >= 1:
        @pl.when((nsub & units) != 0)
        def _(units=units):
            start = (nsub & (-2 * units)) * MOE_SUB
            piece(pl.multiple_of(start, units * MOE_SUB), units * MOE_SUB)

        units //= 2


MOE_FETCH = MOE_TM // (D_FF // MOE_TF)


def _moe_up_kernel(be_ref, nsub_ref, rows0_ref, rowsn_ref, h2_hbm, wg_ref, wu_ref, bg_ref, bu_ref,
                   act_ref, w_scr, xs_buf, stage, sem):
    del be_ref
    b, f = pl.program_id(0), pl.program_id(1)
    nb, nf = pl.num_programs(0), pl.num_programs(1)
    nsub = nsub_ref[b]
    tf = MOE_TF
    cur = b % 2
    first = jnp.logical_and(b == 0, f == 0)
    n_piece_rows = MOE_FETCH * ROW_CHUNKS

    def start_chunk(rows_ref, base):
        for r in range(MOE_FETCH):
            src = h2_hbm.at[pl.ds(rows_ref[0, base + r], ROW_CHUNKS), :]
            dst = stage.at[pl.ds(r * ROW_PITCH, ROW_CHUNKS), :]
            pltpu.make_async_copy(src, dst, sem.at[0]).start(priority=r % 2)

    def wait_chunk():
        pltpu.make_async_copy(h2_hbm.at[pl.ds(0, n_piece_rows), :],
                              stage.at[pl.ds(0, n_piece_rows), :], sem.at[0]).wait()

    def finish_chunk(slot, chunk):
        wait_chunk()
        row0 = pl.multiple_of(chunk * MOE_FETCH, 16)
        for c in range(ROW_CHUNKS):
            piece = stage[pl.ds(c, MOE_FETCH, stride=ROW_PITCH), :]
            xs_buf[slot, pl.ds(row0, MOE_FETCH), c * 128:(c + 1) * 128] = piece.astype(BF16)

    @pl.when(first)
    def _():
        def body(c, carry):
            start_chunk(rows0_ref, c * MOE_FETCH)
            finish_chunk(0, c)
            return carry

        lax.fori_loop(0, nf, body, 0)

    @pl.when(jnp.logical_not(first))
    def _():
        finish_chunk(jnp.where(f == 0, cur, 1 - cur), jnp.where(f == 0, nf - 1, f - 1))

    def prologue():
        start_chunk(rowsn_ref, 0)
        w_scr[:, 0:tf] = wg_ref[...].astype(BF16)
        w_scr[:, tf:2 * tf] = wu_ref[...].astype(BF16)

    def piece(row0, size):
        rows = pl.ds(row0, size)
        gu = jnp.dot(xs_buf[cur, rows, :], w_scr[...], preferred_element_type=F32)
        g = jnp.minimum(gu[:, 0:tf] + bg_ref[...], SWIGLU_LIMIT)
        u = jnp.clip(gu[:, tf:2 * tf] + bu_ref[...], -SWIGLU_LIMIT, SWIGLU_LIMIT)
        act = (u + 1.0) * (g * jax.nn.sigmoid(SWIGLU_ALPHA * g))
        act_ref[rows, :] = act.astype(BF16)

    _for_row_pieces(nsub, piece, prologue)

    def fill(j, carry):
        rows = pl.ds(pl.multiple_of(j * MOE_SUB, MOE_SUB), MOE_SUB)
        act_ref[rows, :] = jnp.zeros((MOE_SUB, tf), BF16)
        return carry

    lax.fori_loop(nsub, MOE_NSUB, fill, 0)

    @pl.when(jnp.logical_and(b == nb - 1, f == nf - 1))
    def _():
        wait_chunk()


def _moe_up(be, nsub, slot_row, h2_all, w_gu, b_gu, n_blocks, n_live):
    d = D_MODEL
    tf = MOE_TF
    nf = D_FF // tf
    last_chunk = n_blocks * nf - 1
    return pl.pallas_call(
        _moe_up_kernel,
        grid_spec=pltpu.PrefetchScalarGridSpec(
            num_scalar_prefetch=2,
            grid=(n_live, nf),
            in_specs=[
                pl.BlockSpec((None, 1, MOE_TM), lambda b, f, be, ns: (0, 0, 0),
                             memory_space=pltpu.SMEM),
                pl.BlockSpec((None, 1, MOE_FETCH),
                             lambda b, f, be, ns: (jnp.minimum((b + 1) * nf + f, last_chunk), 0, 0),
                             memory_space=pltpu.SMEM),
                pl.BlockSpec(memory_space=pl.ANY),
                pl.BlockSpec((None, d, tf), lambda b, f, be, ns: (be[b], 0, f)),
                pl.BlockSpec((None, d, tf), lambda b, f, be, ns: (be[b], 0, nf + f)),
                pl.BlockSpec((None, 1, tf), lambda b, f, be, ns: (be[b], 0, f)),
                pl.BlockSpec((None, 1, tf), lambda b, f, be, ns: (be[b], 0, nf + f)),
            ],
            out_specs=pl.BlockSpec((MOE_TM, tf), lambda b, f, be, ns: (b, f)),
            scratch_shapes=[
                pltpu.VMEM((d, 2 * tf), BF16),
                pltpu.VMEM((2, MOE_TM, d), BF16),
                pltpu.VMEM((MOE_FETCH * ROW_PITCH, 128), F32),
                pltpu.SemaphoreType.DMA((1,)),
            ],
        ),
        out_shape=jax.ShapeDtypeStruct(((n_blocks + 1) * MOE_TM, D_FF), BF16),
        compiler_params=_params(56, 2),
        name="moe_up",
    )(be, nsub, slot_row.reshape(n_blocks, 1, MOE_TM), slot_row.reshape(n_blocks * nf, 1, MOE_FETCH),
      h2_all, w_gu, w_gu, b_gu, b_gu)


def _moe_down_kernel(be_ref, nsub_ref, bmap_ref, act_ref, wd_ref, bd_ref, ys_ref, w_scr):
    del be_ref, bmap_ref
    nsub = nsub_ref[pl.program_id(0)]

    @pl.when(nsub > 0)
    def _():
        def cast_weights():
            w_scr[...] = wd_ref[...].astype(BF16)

        def piece(row0, size):
            rows = pl.ds(row0, size)
            y = jnp.dot(act_ref[rows, :], w_scr[...], preferred_element_type=F32)
            ys_ref[rows, :] = y + bd_ref[...]

        _for_row_pieces(nsub, piece, cast_weights)

        def fill(j, carry):
            rows = pl.ds(pl.multiple_of(j * MOE_SUB, MOE_SUB), MOE_SUB)
            ys_ref[rows, :] = jnp.zeros((MOE_SUB, MOE_TN), F32)
            return carry

        lax.fori_loop(nsub, MOE_NSUB, fill, 0)


def _moe_down(be, nsub, bmap, act, w_down, b_down, n_blocks, n_live):
    dff = act.shape[1]
    tn = MOE_TN
    nn = D_MODEL // tn
    live = lambda b, ns: ns[b] > 0
    return pl.pallas_call(
        _moe_down_kernel,
        grid_spec=pltpu.PrefetchScalarGridSpec(
            num_scalar_prefetch=3,
            grid=(n_live, nn),
            in_specs=[
                pl.BlockSpec((MOE_TM, dff), lambda b, n, be, ns, bm: (bm[b], 0)),
                pl.BlockSpec((None, dff, tn),
                             lambda b, n, be, ns, bm: (be[b], 0, jnp.where(live(b, ns), n, nn - 1))),
                pl.BlockSpec((None, 1, tn),
                             lambda b, n, be, ns, bm: (be[b], 0, jnp.where(live(b, ns), n, nn - 1))),
            ],
            out_specs=pl.BlockSpec(
                (MOE_TM, tn),
                lambda b, n, be, ns, bm: (jnp.where(live(b, ns), b, n_blocks),
                                          jnp.where(live(b, ns), n, 0))),
            scratch_shapes=[pltpu.VMEM((dff, tn), BF16)],
        ),
        out_shape=jax.ShapeDtypeStruct(((n_blocks + 1) * MOE_TM, D_MODEL), F32),
        compiler_params=_params(56, 2),
        name="moe_down",
    )(be, nsub, bmap, act, w_down, b_down)


def _combine_kernel(dest0_ref, destn_ref, x_ref, gate_ref, ga_ref, gf_ref, ys_hbm, o_ref, buf, sem):
    i = pl.program_id(0)
    slot = i % 2

    def issue(dest_ref, slot):
        def body(r8, carry):
            for u in range(8):
                r = r8 * 8 + u
                src = ys_hbm.at[pl.ds(dest_ref[0, r], 1), :]
                dst = buf.at[slot, pl.ds(r, 1), :]
                pltpu.make_async_copy(src, dst, sem.at[slot]).start(priority=u % 2)
            return carry

        lax.fori_loop(0, GATHER_ROWS // 8, body, 0)

    @pl.when(i == 0)
    def _():
        issue(dest0_ref, 0)

    @pl.when(i + 1 < pl.num_programs(0))
    def _():
        issue(destn_ref, 1 - slot)

    pltpu.make_async_copy(ys_hbm.at[pl.ds(0, GATHER_ROWS), :], buf.at[slot], sem.at[slot]).wait()

    acc = gate_ref[:, 0:1] * buf[slot, 0:COMBINE_TOK, :]
    for k in range(1, TOP_K):
        acc = acc + gate_ref[:, k:k + 1] * buf[slot, k * COMBINE_TOK:(k + 1) * COMBINE_TOK, :]
    x2 = x_ref[...] + ga_ref[...] * acc
    o_ref[...] = _rmsnorm(x2, gf_ref[...])


def _combine(dest_tiles, x1, gates, ga, g_final, ys, *, rows_per_seq, per_row):
    m, d = x1.shape
    tm = COMBINE_TOK
    nt = m // tm
    ga_spec = (pl.BlockSpec((tm, d), lambda i: (i, 0)) if per_row else
               pl.BlockSpec((None, 1, d), lambda i: ((i * tm) // rows_per_seq, 0, 0)))
    return pl.pallas_call(
        _combine_kernel,
        grid=(nt,),
        in_specs=[
            pl.BlockSpec((None, 1, GATHER_ROWS), lambda i: (0, 0, 0), memory_space=pltpu.SMEM),
            pl.BlockSpec((None, 1, GATHER_ROWS), lambda i: (jnp.minimum(i + 1, nt - 1), 0, 0),
                         memory_space=pltpu.SMEM),
            pl.BlockSpec((tm, d), lambda i: (i, 0)),
            pl.BlockSpec((tm, 128), lambda i: (i, 0)),
            ga_spec,
            pl.BlockSpec((1, d), lambda i: (0, 0)),
            pl.BlockSpec(memory_space=pl.ANY),
        ],
        out_specs=pl.BlockSpec((tm, d), lambda i: (i, 0)),
        out_shape=jax.ShapeDtypeStruct((m, d), F32),
        scratch_shapes=[pltpu.VMEM((2, GATHER_ROWS, d), F32), pltpu.SemaphoreType.DMA((2,))],
        compiler_params=_params(32, 1),
        name="combine",
    )(dest_tiles, dest_tiles, x1, gates, ga, g_final.reshape(1, d), ys)


def _routing_tables(counts, idx, rank, n_blocks):
    nb = (counts + MOE_TM - 1) // MOE_TM
    cum_nb = jnp.cumsum(nb)
    blk_start = cum_nb - nb
    total = cum_nb[-1]
    dest = (blk_start * MOE_TM)[idx] + rank

    b_ids = jnp.arange(n_blocks, dtype=I32)
    bmap = jnp.maximum(jnp.minimum(b_ids, total - 1), 0)
    be = jnp.minimum(jnp.sum(bmap[:, None] >= cum_nb[None, :], axis=1), N_EXPERTS - 1).astype(I32)
    rows = jnp.clip(counts[be] - (b_ids - blk_start[be]) * MOE_TM, 0, MOE_TM)
    rows = jnp.where(b_ids < total, rows, 0)
    nsub = ((rows + MOE_SUB - 1) // MOE_SUB).astype(I32)
    return dest, be, nsub, bmap.astype(I32), total.astype(I32)


def kernel(x_prompt, x_sample, state_pool, state_conv, c_prompt, c_sample, w_ada, b_ada, g_norm1,
           w_in, w_pool, pool_scale, w_conv, b_conv, ln_g, ln_b, w_out, g_norm2, w_router,
           b_router, w_gu, b_gu, w_down, b_down, g_final):
    assert w_ada.shape[0] == 1, "single-layer trunk"
    nbp, seq, d = x_prompt.shape
    nbs, dec_seq, _ = x_sample.shape
    assert dec_seq == 1
    past_len = 16384
    tp, ts = nbp * seq, nbs
    t_all = tp + ts

    n_c = nbp + nbs
    pad_c = (-n_c) % 16
    c_all = jnp.concatenate([c_prompt, c_sample, jnp.zeros((pad_c, d), F32)], axis=0)
    mod = _ada(c_all, w_ada[0], b_ada[0])
    mod_p = [mod[:nbp, k * d:(k + 1) * d].reshape(nbp, 1, d) for k in range(N_MOD)]
    mod_s = [mod[nbp:n_c, k * d:(k + 1) * d] for k in range(N_MOD)]

    w_in_bf = w_in[0].astype(BF16)
    w_out_bf = w_out[0].astype(BF16)
    w_pool_bf = w_pool[0].astype(BF16)
    wr_bf = w_router[0].astype(BF16)
    row = lambda a: a.reshape(1, -1)

    xp = x_prompt.reshape(tp, d)
    xs_tok = x_sample.reshape(ts, d)
    tm = 512

    z_p = _in_proj(xp, row(g_norm1[0]), mod_p[1], mod_p[0], w_in_bf,
                   tm=tm, rows_per_seq=seq, per_row=False)
    z_s = _in_proj(xs_tok, row(g_norm1[0]), mod_s[1], mod_s[0], w_in_bf,
                   tm=ts, rows_per_seq=1, per_row=True)
    mix_p, npool_p, nconv_p = _mixer_seq(
        z_p, jnp.zeros((nbp, POOL_BUF, POOL_W), F32), jnp.zeros((nbp, CONV_BUF, CONV_W), F32),
        w_pool_bf, pool_scale[0], w_conv[0], b_conv[0], ln_g[0], ln_b[0],
        seq_len=seq, start_pos=0)
    mix_s, u_s, a_s = _mixer_step(
        z_s, state_pool[0], state_conv[0], w_pool_bf, pool_scale[0], w_conv[0], b_conv[0],
        ln_g[0], ln_b[0], start_pos=past_len)
    x1_p = _out_proj(mix_p, w_out_bf, xp, mod_p[2], tm=tm, rows_per_seq=seq, per_row=False)
    x1_s = _out_proj(mix_s, w_out_bf, xs_tok, mod_s[2], tm=ts, rows_per_seq=1, per_row=True)

    h2_all = jnp.zeros((t_all * ROW_PITCH, 128), F32)
    counts0 = jnp.zeros((1, N_EXPERTS), F32)
    h2_all, meta_p, gate_p, counts1 = _route(
        x1_p, row(g_norm2[0]), mod_p[4], mod_p[3], wr_bf, row(b_router[0]), counts0, h2_all,
        tm=tm, rows_per_seq=seq, per_row=False, row_block0=0)
    h2_all, meta_s, gate_s, counts2 = _route(
        x1_s, row(g_norm2[0]), mod_s[4], mod_s[3], wr_bf, row(b_router[0]), counts1, h2_all,
        tm=ts, rows_per_seq=1, per_row=True, row_block0=tp // ts)

    n_assign = t_all * TOP_K
    n_blocks = n_assign // MOE_TM + N_EXPERTS
    counts = counts2[0].astype(I32)
    idx = jnp.concatenate([meta_p[:, 0:TOP_K], meta_s[:, 0:TOP_K]], axis=0)
    rank = jnp.concatenate([meta_p[:, TOP_K:2 * TOP_K], meta_s[:, TOP_K:2 * TOP_K]], axis=0)
    dest, be, nsub, bmap, n_live = _routing_tables(counts, idx, rank, n_blocks)

    tok_rows = jnp.broadcast_to((jnp.arange(t_all, dtype=I32) * ROW_PITCH)[:, None],
                                (t_all, TOP_K))
    slot_row = jnp.zeros((n_blocks * MOE_TM,), I32).at[dest.reshape(-1)].set(tok_rows.reshape(-1))

    act = _moe_up(be, nsub, slot_row, h2_all, w_gu[0], b_gu[0].reshape(N_EXPERTS, 1, 2 * D_FF),
                  n_blocks, n_live)
    ys = _moe_down(be, nsub, bmap, act, w_down[0], b_down[0].reshape(N_EXPERTS, 1, d), n_blocks,
                   n_live)

    def dest_tiles(dst):
        n = dst.shape[0] // COMBINE_TOK
        return dst.reshape(n, COMBINE_TOK, TOP_K).transpose(0, 2, 1).reshape(n, 1, GATHER_ROWS)

    y_p = _combine(dest_tiles(dest[:tp]), x1_p, gate_p, mod_p[5], g_final, ys,
                   rows_per_seq=seq, per_row=False)
    y_s = _combine(dest_tiles(dest[tp:]), x1_s, gate_s, mod_s[5], g_final, ys,
                   rows_per_seq=1, per_row=True)

    new_pool_s = jnp.concatenate([state_pool[0][:, 1:], u_s[:, None, :]], axis=1)
    new_conv_s = jnp.concatenate([state_conv[0][:, 1:], a_s[:, None, :]], axis=1)
    return (y_p.reshape(nbp, seq, d), y_s.reshape(nbs, 1, d), npool_p[None], nconv_p[None],
            new_pool_s[None], new_conv_s[None])
```
